```python
import math
import jax, jax.numpy as jnp
from jax import lax
import numpy as np

D_MODEL = 2048
BATCH = 8
SEQ = 2048
DEPTH = 2
DEC_BATCH = 32
DEC_SEQ = 32
PAST_LEN = 4096

CHUNK = 64
MIX_WIDTH = D_MODEL
POOL_WINDOWS = (2, 4, 8, 16)
N_POOL_GROUPS = 4
POOL_WIDTH = MIX_WIDTH // 4
POOL_GROUP = POOL_WIDTH // N_POOL_GROUPS
POOL_HIST = max(POOL_WINDOWS) - 1
DN_HEAD_DIM = 128
DN_WIDTH = MIX_WIDTH - POOL_WIDTH
N_DN_HEADS = DN_WIDTH // DN_HEAD_DIM
QKV_WIDTH = 3 * DN_WIDTH
CONV_W = 4
IN_WIDTH = POOL_WIDTH + QKV_WIDTH + DN_WIDTH + 2 * N_DN_HEADS
N_EXPERTS = 32
TOP_K = 4
D_EXPERT = D_MODEL
SWIGLU_LIMIT = 7.0
SWIGLU_ALPHA = 1.702
DEEPNORM_ALPHA = (2 * DEPTH) ** 0.25
DEEPNORM_BETA = (8 * DEPTH) ** -0.25
LN_EPS = 1e-5
RMS_EPS = 1e-6
L2_EPS = 1e-6

kernel_name = 'hybrid_pool_gdn_moe_stream_step'


def layer_norm(x, g, b):
    xf = x.astype(jnp.float32)
    xc = xf - jnp.mean(xf, -1, keepdims=True)
    var = jnp.mean(xc * xc, -1, keepdims=True)
    return (xc * lax.rsqrt(var + LN_EPS) * g.astype(jnp.float32) + b.astype(jnp.float32)).astype(x.dtype)


def l2norm(x):
    return x * lax.rsqrt(jnp.sum(x * x, -1, keepdims=True) + L2_EPS)


def causal_pool(u, hist, pos0, pool_w, pool_scale):
    B, L, _ = u.shape
    uf = jnp.concatenate([hist.astype(u.dtype), u], axis=1).astype(jnp.float32)
    cs = jnp.pad(jnp.cumsum(uf, axis=1), ((0, 0), (1, 0), (0, 0)))
    pos = pos0 + jnp.arange(L)
    means = []
    for gi, w in enumerate(POOL_WINDOWS):
        sl = slice(gi * POOL_GROUP, (gi + 1) * POOL_GROUP)
        s = cs[:, POOL_HIST + 1:POOL_HIST + 1 + L, sl] - cs[:, POOL_HIST + 1 - w:POOL_HIST + 1 - w + L, sl]
        cnt = jnp.minimum(pos + 1, w).astype(jnp.float32)
        means.append(s / cnt[None, :, None])
    d = (jnp.concatenate(means, -1) - u.astype(jnp.float32)).reshape(B, L, N_POOL_GROUPS, POOL_GROUP)
    y = jnp.einsum('blgc,gcd->blgd', d, pool_w.astype(jnp.float32)).reshape(B, L, POOL_WIDTH)
    return (y * pool_scale.astype(jnp.float32)).astype(u.dtype)


def short_conv(u, hist, w):
    ext = jnp.concatenate([hist.astype(u.dtype), u], axis=1)
    out = lax.conv_general_dilated(ext, w.astype(u.dtype)[:, None, :], (1,), 'VALID',
                                   dimension_numbers=('NWC', 'WIO', 'NWC'),
                                   feature_group_count=u.shape[-1])
    return jax.nn.silu(out), ext[:, -(CONV_W - 1):]


def gated_delta(q, k, v, beta, g, s0):
    B, L, H, DK = q.shape
    DV = v.shape[-1]
    C = CHUNK if L % CHUNK == 0 else L
    N = L // C
    blk4 = lambda t: t.reshape(B, N, C, H, t.shape[-1]).transpose(1, 0, 3, 2, 4)
    blk3 = lambda t: t.reshape(B, N, C, H).transpose(1, 0, 3, 2)
    idx = jnp.arange(C)
    incl = idx[:, None] >= idx[None, :]
    strict = idx[:, None] > idx[None, :]
    eye = jnp.eye(C, dtype=jnp.float32)

    def step(s, blk):
        qc, kc, vc, bc, gr = blk
        gc = jnp.cumsum(gr, -1)
        dec = jnp.exp(jnp.where(incl, gc[..., :, None] - gc[..., None, :], -jnp.inf))
        kk = jnp.einsum('bhcd,bhed->bhce', kc, kc)
        a = eye + bc[..., :, None] * kk * jnp.where(strict, dec, 0.0)
        eg = jnp.exp(gc)
        rhs = bc[..., None] * (vc - eg[..., None] * jnp.einsum('bhcd,bhdv->bhcv', kc, s))
        u = lax.linalg.triangular_solve(a, rhs, left_side=True, lower=True, unit_diagonal=True)
        qk = jnp.einsum('bhcd,bhed->bhce', qc, kc) * dec
        o = eg[..., None] * jnp.einsum('bhcd,bhdv->bhcv', qc, s) + jnp.einsum('bhce,bhev->bhcv', qk, u)
        tail = jnp.exp(gc[..., -1:] - gc)
        s_new = jnp.exp(gc[..., -1])[..., None, None] * s + jnp.einsum('bhcd,bhcv->bhdv', kc * tail[..., None], u)
        return s_new, o

    s_fin, o = lax.scan(step, s0.astype(jnp.float32),
                        (blk4(q), blk4(k), blk4(v), blk3(beta), blk3(g)))
    return o.transpose(1, 0, 3, 2, 4).reshape(B, L, H, DV), s_fin


def mixer(h, hist_pool, hist_conv, s0, pos0, w_in, conv_w, pool_w, pool_scale, a_log, dt_bias, dn_norm_w, w_out):
    B, L, _ = h.shape
    z = h @ w_in
    o1 = POOL_WIDTH
    o2 = o1 + QKV_WIDTH
    o3 = o2 + DN_WIDTH
    o4 = o3 + N_DN_HEADS
    pool_in, qkv_in, gate, b_in, a_in = z[..., :o1], z[..., o1:o2], z[..., o2:o3], z[..., o3:o4], z[..., o4:]
    y_pool = causal_pool(pool_in, hist_pool, pos0, pool_w, pool_scale)
    new_pool = jnp.concatenate([hist_pool.astype(pool_in.dtype), pool_in], axis=1)[:, -POOL_HIST:]
    qkv, new_conv = short_conv(qkv_in, hist_conv, conv_w)
    q, k, v = jnp.split(qkv.astype(jnp.float32), 3, axis=-1)
    q = l2norm(q.reshape(B, L, N_DN_HEADS, DN_HEAD_DIM)) * (DN_HEAD_DIM ** -0.5)
    k = l2norm(k.reshape(B, L, N_DN_HEADS, DN_HEAD_DIM))
    v = v.reshape(B, L, N_DN_HEADS, DN_HEAD_DIM)
    beta = jax.nn.sigmoid(b_in.astype(jnp.float32))
    g = -jnp.exp(a_log.astype(jnp.float32)) * jax.nn.softplus(a_in.astype(jnp.float32) + dt_bias.astype(jnp.float32))
    o, s_new = gated_delta(q, k, v, beta, g, s0)
    o = o * lax.rsqrt(jnp.mean(o * o, -1, keepdims=True) + RMS_EPS) * dn_norm_w.astype(jnp.float32)
    o = o * jax.nn.silu(gate.reshape(B, L, N_DN_HEADS, DN_HEAD_DIM).astype(jnp.float32))
    mix = jnp.concatenate([y_pool, o.reshape(B, L, DN_WIDTH).astype(y_pool.dtype)], axis=-1)
    return mix @ w_out, new_pool, new_conv, s_new.astype(h.dtype)


def moe(h, w_router, b_router, w1, b1, w2, b2):
    B, L, D = h.shape
    t = h.reshape(B * L, D)
    logits = (t @ w_router).astype(jnp.float32) + b_router.astype(jnp.float32)
    top_v, top_i = lax.top_k(logits, TOP_K)
    gates = jax.nn.softmax(top_v, axis=-1)
    dense_g = jnp.sum(jax.nn.one_hot(top_i, N_EXPERTS, dtype=jnp.float32) * gates[..., None], axis=1)

    def expert(acc, p):
        w1e, b1e, w2e, b2e, ge = p
        u = t @ w1e + b1e
        x_glu = jnp.minimum(u[:, :D_EXPERT], SWIGLU_LIMIT)
        x_lin = jnp.clip(u[:, D_EXPERT:], -SWIGLU_LIMIT, SWIGLU_LIMIT)
        act = x_glu * jax.nn.sigmoid(SWIGLU_ALPHA * x_glu) * (x_lin + 1)
        y = act @ w2e + b2e
        return acc + ge[:, None] * y.astype(jnp.float32), None

    acc, _ = lax.scan(expert, jnp.zeros((B * L, D), jnp.float32), (w1, b1, w2, b2, dense_g.T))
    return acc.reshape(B, L, D).astype(h.dtype)


def trunk_layer(x, hist_pool, hist_conv, s0, pos0, w_in, conv_w, pool_w, pool_scale, a_log, dt_bias,
                dn_norm_w, w_out, ln1_g, ln1_b, w_router, b_router, w1, b1, w2, b2, ln2_g, ln2_b):
    m, new_pool, new_conv, s_new = mixer(x, hist_pool, hist_conv, s0, pos0, w_in, conv_w, pool_w, pool_scale,
                                         a_log, dt_bias, dn_norm_w, w_out)
    x = layer_norm(DEEPNORM_ALPHA * x + m, ln1_g, ln1_b)
    x = layer_norm(DEEPNORM_ALPHA * x + moe(x, w_router, b_router, w1, b1, w2, b2), ln2_g, ln2_b)
    return x, new_pool, new_conv, s_new


def setup_inputs(seed: int = 0) -> dict:
    key = jax.random.key(seed)
    ks = jax.random.split(key, 24)
    f32 = jnp.float32
    nrm = lambda k, shape, s: jax.random.normal(k, shape, f32) * s
    dt = jnp.exp(jax.random.uniform(ks[9], (DEPTH, N_DN_HEADS), f32, math.log(1e-3), math.log(1e-1)))
    return {
        'x_prompt': nrm(ks[0], (BATCH, SEQ, D_MODEL), 1.0),
        'x_sample': nrm(ks[1], (DEC_BATCH, DEC_SEQ, D_MODEL), 1.0),
        'state_pool': nrm(ks[2], (DEPTH, DEC_BATCH, POOL_HIST, POOL_WIDTH), 1.0),
        'state_conv': nrm(ks[3], (DEPTH, DEC_BATCH, CONV_W - 1, QKV_WIDTH), 1.0),
        'state_delta': nrm(ks[4], (DEPTH, DEC_BATCH, N_DN_HEADS, DN_HEAD_DIM, DN_HEAD_DIM), 0.1),
        'w_in': nrm(ks[5], (DEPTH, D_MODEL, IN_WIDTH), D_MODEL ** -0.5),
        'conv_w': nrm(ks[6], (DEPTH, CONV_W, QKV_WIDTH), CONV_W ** -0.5),
        'pool_w': nrm(ks[7], (DEPTH, N_POOL_GROUPS, POOL_GROUP, POOL_GROUP), POOL_GROUP ** -0.5),
        'pool_scale': 1.0 + nrm(ks[8], (DEPTH, POOL_WIDTH), 0.1),
        'a_log': jnp.log(jax.random.uniform(ks[10], (DEPTH, N_DN_HEADS), f32, 1.0, 16.0)),
        'dt_bias': dt + jnp.log(-jnp.expm1(-dt)),
        'dn_norm_w': 1.0 + nrm(ks[11], (DEPTH, DN_HEAD_DIM), 0.1),
        'w_out': nrm(ks[12], (DEPTH, MIX_WIDTH, D_MODEL), DEEPNORM_BETA * MIX_WIDTH ** -0.5),
        'ln1_g': 1.0 + nrm(ks[13], (DEPTH, D_MODEL), 0.1),
        'ln1_b': nrm(ks[14], (DEPTH, D_MODEL), 0.02),
        'w_router': nrm(ks[15], (DEPTH, D_MODEL, N_EXPERTS), D_MODEL ** -0.5),
        'b_router': nrm(ks[16], (DEPTH, N_EXPERTS), 0.01),
        'w1': nrm(ks[17], (DEPTH, N_EXPERTS, D_MODEL, 2 * D_EXPERT), D_MODEL ** -0.5),
        'b1': nrm(ks[18], (DEPTH, N_EXPERTS, 2 * D_EXPERT), 0.02),
        'w2': nrm(ks[19], (DEPTH, N_EXPERTS, D_EXPERT, D_MODEL), DEEPNORM_BETA * D_EXPERT ** -0.5),
        'b2': nrm(ks[20], (DEPTH, N_EXPERTS, D_MODEL), 0.02),
        'ln2_g': 1.0 + nrm(ks[21], (DEPTH, D_MODEL), 0.1),
        'ln2_b': nrm(ks[22], (DEPTH, D_MODEL), 0.02),
    }


def reference(x_prompt, x_sample, state_pool, state_conv, state_delta, w_in, conv_w, pool_w, pool_scale,
              a_log, dt_bias, dn_norm_w, w_out, ln1_g, ln1_b, w_router, b_router, w1, b1, w2, b2, ln2_g, ln2_b):
    yp, ys = x_prompt, x_sample
    bp = x_prompt.shape[0]
    pp, pc, pd, sp, sc, sd = [], [], [], [], [], []
    for l in range(DEPTH):
        lw = (w_in[l], conv_w[l], pool_w[l], pool_scale[l], a_log[l], dt_bias[l], dn_norm_w[l], w_out[l],
              ln1_g[l], ln1_b[l], w_router[l], b_router[l], w1[l], b1[l], w2[l], b2[l], ln2_g[l], ln2_b[l])
        zp = jnp.zeros((bp, POOL_HIST, POOL_WIDTH), x_prompt.dtype)
        zc = jnp.zeros((bp, CONV_W - 1, QKV_WIDTH), x_prompt.dtype)
        zs = jnp.zeros((bp, N_DN_HEADS, DN_HEAD_DIM, DN_HEAD_DIM), jnp.float32)
        yp, a1, a2, a3 = trunk_layer(yp, zp, zc, zs, 0, *lw)
        ys, b1_, b2_, b3_ = trunk_layer(ys, state_pool[l], state_conv[l], state_delta[l], PAST_LEN, *lw)
        pp.append(a1); pc.append(a2); pd.append(a3)
        sp.append(b1_); sc.append(b2_); sd.append(b3_)
    return (yp, ys, jnp.stack(pp), jnp.stack(pc), jnp.stack(pd), jnp.stack(sp), jnp.stack(sc), jnp.stack(sd))
```

```python
import functools

import jax
import jax.numpy as jnp
from jax import lax
from jax.experimental import pallas as pl
from jax.experimental.pallas import tpu as pltpu

F32 = jnp.float32
BF16 = jnp.bfloat16
HIGHEST = lax.Precision.HIGHEST

POOL_WINDOWS = (2, 4, 8, 16)
POOL_GROUP = 128
POOL_WIDTH = POOL_GROUP * len(POOL_WINDOWS)
POOL_HIST = max(POOL_WINDOWS) - 1
HEAD_DIM = 128
N_HEADS = 12
DN_WIDTH = N_HEADS * HEAD_DIM
QKV_WIDTH = 3 * DN_WIDTH
CONV_W = 4
MAIN_WIDTH = POOL_WIDTH + QKV_WIDTH + DN_WIDTH
GATE_COL = POOL_WIDTH + QKV_WIDTH
CHUNK = 64
PAST_LEN = 4096
TOP_K = 4
SWIGLU_LIMIT = 7.0
SWIGLU_ALPHA = 1.702
LN_EPS = 1e-5
RMS_EPS = 1e-6
L2_EPS = 1e-6

LANES = 128
SOLVE_BLOCK = 16
POOL_PAD = 16
CONV_PAD = 8
VMEM_LIMIT = 56 * 1024 * 1024


def _cparams(*sem):
    return pltpu.CompilerParams(dimension_semantics=sem, vmem_limit_bytes=VMEM_LIMIT)


def _matmul_kernel(x_ref, w_ref, o_ref):
    o_ref[...] = jnp.dot(x_ref[...], w_ref[...], preferred_element_type=F32).astype(o_ref.dtype)


def _matmul(x, w, bm, bn, out_dtype=F32):
    m, k = x.shape
    n = w.shape[1]
    return pl.pallas_call(
        _matmul_kernel,
        grid=(m // bm, n // bn),
        in_specs=[pl.BlockSpec((bm, k), lambda i, j: (i, 0)), pl.BlockSpec((k, bn), lambda i, j: (0, j))],
        out_specs=pl.BlockSpec((bm, bn), lambda i, j: (i, j)),
        out_shape=jax.ShapeDtypeStruct((m, n), out_dtype),
        compiler_params=_cparams("parallel", "arbitrary"),
        name="in_proj",
    )(x, w)


def _unit_lower_inverse(l_mat, c):
    r = lax.broadcasted_iota(jnp.int32, (c, c), 0)
    q = lax.broadcasted_iota(jnp.int32, (c, c), 1)
    eye = (r == q).astype(F32)
    same = (r // SOLVE_BLOCK) == (q // SOLVE_BLOCK)
    l_d = jnp.where(same, l_mat, 0.0)
    e = jnp.where(same, 0.0, l_mat)
    dot = functools.partial(jnp.dot, preferred_element_type=F32, precision=HIGHEST)
    p = l_d
    t_d = eye - l_d
    steps = SOLVE_BLOCK.bit_length() - 2
    for _ in range(steps):
        p = dot(p, p)
        t_d = dot(t_d, eye + p)
    n = dot(t_d, e)
    nb = c // SOLVE_BLOCK
    if nb == 1:
        return t_d
    if nb == 2:
        return t_d - dot(n, t_d)
    assert nb == 4, nb
    n2 = dot(n, n)
    return dot(dot(eye - n, eye + n2), t_d)


def _seq_mixer_kernel(z_ref, zs_ref, hp_ref, hc_ref, s0_ref, convw_ref, poolw_ref, pscale_ref, alog_ref, dtb_ref,
                      nw_ref, mix_ref, s_ref, pext, cext, *, c, pos0):
    ci = pl.program_id(1)

    @pl.when(ci == 0)
    def _():
        pext[0:POOL_PAD, :] = hp_ref[0]
        cext[0:CONV_PAD, :] = hc_ref[0]
        s_ref[...] = s0_ref[...]

    pext[POOL_PAD:POOL_PAD + c, :] = z_ref[:, 0:POOL_WIDTH]
    cext[CONV_PAD:CONV_PAD + c, :] = z_ref[:, POOL_WIDTH:GATE_COL]

    pos = lax.broadcasted_iota(jnp.int32, (c, 1), 0) + ci * c + pos0
    for gi, w in enumerate(POOL_WINDOWS):
        cols = slice(gi * POOL_GROUP, (gi + 1) * POOL_GROUP)
        u = pext[POOL_PAD:POOL_PAD + c, cols]
        s = u
        for j in range(1, w):
            s = s + pext[POOL_PAD - j:POOL_PAD - j + c, cols]
        cnt = jnp.minimum(pos + 1, w).astype(F32)
        d = s / cnt - u
        y = jnp.dot(d.astype(BF16), poolw_ref[gi], preferred_element_type=F32)
        mix_ref[:, cols] = (y * pscale_ref[:, cols]).astype(mix_ref.dtype)

    zs = zs_ref[...]
    beta_all = jax.nn.sigmoid(zs)
    a = zs + dtb_ref[...]
    softplus = jnp.maximum(a, 0.0) + jnp.log1p(jnp.exp(-jnp.abs(a)))
    g_all = -jnp.exp(alog_ref[...]) * softplus
    r = lax.broadcasted_iota(jnp.int32, (c, c), 0)
    q_i = lax.broadcasted_iota(jnp.int32, (c, c), 1)
    incl = r >= q_i
    strict = r > q_i
    gc_all = jnp.dot(incl.astype(F32), g_all, preferred_element_type=F32, precision=HIGHEST)
    sel = (lax.broadcasted_iota(jnp.int32, (16, LANES), 1)
           == lax.broadcasted_iota(jnp.int32, (16, LANES), 0) + N_HEADS).astype(F32)
    gc_rows = lax.dot_general(sel, gc_all, (((1,), (1,)), ((), ())), preferred_element_type=F32,
                              precision=HIGHEST)

    def conv_block(col0):
        cols = slice(col0, col0 + HEAD_DIM)
        acc = convw_ref[0:1, cols] * cext[CONV_PAD - (CONV_W - 1):CONV_PAD - (CONV_W - 1) + c, cols]
        for j in range(1, CONV_W):
            lo = CONV_PAD - (CONV_W - 1) + j
            acc = acc + convw_ref[j:j + 1, cols] * cext[lo:lo + c, cols]
        return acc * jax.nn.sigmoid(acc)

    nt = (((1,), (1,)), ((), ()))
    tn = (((0,), (0,)), ((), ()))
    for h in range(N_HEADS):
        qh = conv_block(h * HEAD_DIM)
        kh = conv_block(DN_WIDTH + h * HEAD_DIM)
        vh = conv_block(2 * DN_WIDTH + h * HEAD_DIM)
        qh = qh * lax.rsqrt(jnp.sum(qh * qh, -1, keepdims=True) + L2_EPS) * (HEAD_DIM ** -0.5)
        kh = kh * lax.rsqrt(jnp.sum(kh * kh, -1, keepdims=True) + L2_EPS)
        beta = beta_all[:, h:h + 1]
        gcol = gc_all[:, N_HEADS + h:N_HEADS + h + 1]
        grow = gc_rows[h:h + 1, :]
        dec = jnp.exp(jnp.where(incl, gcol - grow, -jnp.inf))
        kb = kh.astype(BF16)
        qb = qh.astype(BF16)
        kk = lax.dot_general(kb, kb, nt, preferred_element_type=F32)
        l_mat = beta * kk * jnp.where(strict, dec, 0.0)
        t_inv = _unit_lower_inverse(l_mat, c)
        eg = jnp.exp(gcol)
        x_cat = jnp.concatenate([beta * vh, (beta * eg) * kh], axis=1)
        uw = jnp.dot(t_inv, x_cat, preferred_element_type=F32, precision=HIGHEST)
        s_h = s_ref[0, h]
        s_b = s_h.astype(BF16)
        wq = jnp.concatenate([uw[:, HEAD_DIM:], qh], axis=0).astype(BF16)
        ws_qs = jnp.dot(wq, s_b, preferred_element_type=F32)
        u = uw[:, :HEAD_DIM] - ws_qs[:c]
        ub = u.astype(BF16)
        qk = lax.dot_general(qb, kb, nt, preferred_element_type=F32) * dec
        o = eg * ws_qs[c:] + jnp.dot(qk.astype(BF16), ub, preferred_element_type=F32)
        g_last = gcol[c - 1:c, :]
        tail = jnp.exp(g_last - gcol)
        s_ref[0, h] = jnp.exp(g_last) * s_h + lax.dot_general((kh * tail).astype(BF16), ub, tn,
                                                               preferred_element_type=F32)
        o = o * lax.rsqrt(jnp.mean(o * o, -1, keepdims=True) + RMS_EPS) * nw_ref[...]
        gate = z_ref[:, GATE_COL + h * HEAD_DIM:GATE_COL + (h + 1) * HEAD_DIM]
        o = o * (gate * jax.nn.sigmoid(gate))
        mix_ref[:, POOL_WIDTH + h * HEAD_DIM:POOL_WIDTH + (h + 1) * HEAD_DIM] = o.astype(mix_ref.dtype)

    pext[0:POOL_PAD, :] = pext[c:c + POOL_PAD, :]
    cext[0:CONV_PAD, :] = cext[c:c + CONV_PAD, :]


def _seq_mixer(z, zs, row0, nseq, seqlen, c, pos0, hist_pool, hist_conv, s0, conv_w, pool_w, pool_scale, a_log_row,
               dt_bias_row, norm_w):
    nchunk = seqlen // c
    blk0 = row0 // c
    d_mix = POOL_WIDTH + DN_WIDTH
    row_map = lambda b, ci: (blk0 + b * nchunk + ci, 0)
    seq_map3 = lambda b, ci: (b, 0, 0)
    const2 = lambda b, ci: (0, 0)
    kern = functools.partial(_seq_mixer_kernel, c=c, pos0=pos0)
    return pl.pallas_call(
        kern,
        grid=(nseq, nchunk),
        in_specs=[
            pl.BlockSpec((c, MAIN_WIDTH), row_map),
            pl.BlockSpec((c, LANES), row_map),
            pl.BlockSpec((1, POOL_PAD, POOL_WIDTH), seq_map3),
            pl.BlockSpec((1, CONV_PAD, QKV_WIDTH), seq_map3),
            pl.BlockSpec((1, N_HEADS, HEAD_DIM, HEAD_DIM), lambda b, ci: (b, 0, 0, 0)),
            pl.BlockSpec((CONV_W, QKV_WIDTH), const2),
            pl.BlockSpec((len(POOL_WINDOWS), POOL_GROUP, POOL_GROUP), lambda b, ci: (0, 0, 0)),
            pl.BlockSpec((1, POOL_WIDTH), const2),
            pl.BlockSpec((1, LANES), const2),
            pl.BlockSpec((1, LANES), const2),
            pl.BlockSpec((1, HEAD_DIM), const2),
        ],
        out_specs=[
            pl.BlockSpec((c, d_mix), lambda b, ci: (b * nchunk + ci, 0)),
            pl.BlockSpec((1, N_HEADS, HEAD_DIM, HEAD_DIM), lambda b, ci: (b, 0, 0, 0)),
        ],
        out_shape=[
            jax.ShapeDtypeStruct((nseq * seqlen, d_mix), BF16),
            jax.ShapeDtypeStruct((nseq, N_HEADS, HEAD_DIM, HEAD_DIM), F32),
        ],
        scratch_shapes=[pltpu.VMEM((POOL_PAD + c, POOL_WIDTH), F32), pltpu.VMEM((CONV_PAD + c, QKV_WIDTH), F32)],
        compiler_params=_cparams("parallel", "arbitrary"),
        name="seq_mixer",
    )(z, zs, hist_pool, hist_conv, s0, conv_w, pool_w, pool_scale, a_log_row, dt_bias_row, norm_w)


def _layer_norm_rows(x, g, b):
    xc = x - jnp.mean(x, -1, keepdims=True)
    var = jnp.mean(xc * xc, -1, keepdims=True)
    return xc * lax.rsqrt(var + LN_EPS) * g + b


def _outproj_ln_router_kernel(mix_ref, wout_ref, x_ref, g_ref, b_ref, wr_ref, br_ref, x1_ref, idx_ref, gate_ref, *,
                              alpha, n_experts):
    m = jnp.dot(mix_ref[...], wout_ref[...], preferred_element_type=F32)
    x1 = _layer_norm_rows(alpha * x_ref[...] + m, g_ref[...], b_ref[...])
    x1_ref[...] = x1
    logits = jnp.dot(x1, wr_ref[...], preferred_element_type=F32, precision=HIGHEST) + br_ref[...]
    lane = lax.broadcasted_iota(jnp.int32, logits.shape, 1).astype(F32)
    logits = jnp.where(lane < n_experts, logits, -jnp.inf)
    vals, idxs = [], []
    for _ in range(TOP_K):
        v = jnp.max(logits, -1, keepdims=True)
        i = jnp.min(jnp.where(logits == v, lane, float(LANES)), -1, keepdims=True)
        vals.append(v)
        idxs.append(i)
        logits = jnp.where(lane == i, -jnp.inf, logits)
    es = [jnp.exp(v - vals[0]) for v in vals]
    denom = es[0]
    for e in es[1:]:
        denom = denom + e
    idx_out = jnp.zeros(logits.shape, F32)
    gate_out = jnp.zeros(logits.shape, F32)
    for k in range(TOP_K):
        idx_out = jnp.where(lane == k, idxs[k], idx_out)
        gate_out = jnp.where(lane == k, es[k] / denom, gate_out)
    idx_ref[...] = idx_out.astype(jnp.int32)
    gate_ref[...] = gate_out


def _outproj_ln_router(mix, w_out, x, ln_g, ln_b, w_router_pad, b_router_pad, alpha, n_experts, bm):
    t, d = x.shape
    row = lambda i: (i, 0)
    const = lambda i: (0, 0)
    kern = functools.partial(_outproj_ln_router_kernel, alpha=alpha, n_experts=n_experts)
    return pl.pallas_call(
        kern,
        grid=(t // bm,),
        in_specs=[
            pl.BlockSpec((bm, mix.shape[1]), row),
            pl.BlockSpec(w_out.shape, const),
            pl.BlockSpec((bm, d), row),
            pl.BlockSpec((1, d), const),
            pl.BlockSpec((1, d), const),
            pl.BlockSpec((d, LANES), const),
            pl.BlockSpec((1, LANES), const),
        ],
        out_specs=[pl.BlockSpec((bm, d), row), pl.BlockSpec((bm, LANES), row), pl.BlockSpec((bm, LANES), row)],
        out_shape=[
            jax.ShapeDtypeStruct((t, d), F32),
            jax.ShapeDtypeStruct((t, LANES), jnp.int32),
            jax.ShapeDtypeStruct((t, LANES), F32),
        ],
        compiler_params=_cparams("parallel"),
        name="outproj_ln_router",
    )(mix, w_out, x, ln_g, ln_b, w_router_pad, b_router_pad)


def _rank_kernel(idx_ref, rank_ref, count_ref, carry):
    @pl.when(pl.program_id(0) == 0)
    def _():
        carry[...] = jnp.zeros_like(carry)

    idx = idx_ref[...]
    bm = idx.shape[0]
    lane = lax.broadcasted_iota(jnp.int32, (bm, LANES), 1)
    onehots = [(idx[:, k:k + 1] == lane) for k in range(TOP_K)]
    cnt = onehots[0].astype(F32)
    for oh in onehots[1:]:
        cnt = cnt + oh.astype(F32)
    r = lax.broadcasted_iota(jnp.int32, (bm, bm), 0)
    q = lax.broadcasted_iota(jnp.int32, (bm, bm), 1)
    before = jnp.dot((r > q).astype(BF16), cnt.astype(BF16), preferred_element_type=F32) + carry[...]
    out = jnp.zeros((bm, LANES), jnp.int32)
    for k in range(TOP_K):
        rk = jnp.sum(jnp.where(onehots[k], before, 0.0), -1, keepdims=True).astype(jnp.int32)
        out = jnp.where(lane == k, rk, out)
    rank_ref[...] = out
    total = carry[...] + jnp.sum(cnt, 0, keepdims=True)
    carry[...] = total
    count_ref[...] = total.astype(jnp.int32)


def _rank(idx, bm):
    t = idx.shape[0]
    return pl.pallas_call(
        _rank_kernel,
        grid=(t // bm,),
        in_specs=[pl.BlockSpec((bm, LANES), lambda i: (i, 0))],
        out_specs=[pl.BlockSpec((bm, LANES), lambda i: (i, 0)), pl.BlockSpec((1, LANES), lambda i: (0, 0))],
        out_shape=[jax.ShapeDtypeStruct((t, LANES), jnp.int32), jax.ShapeDtypeStruct((1, LANES), jnp.int32)],
        scratch_shapes=[pltpu.VMEM((1, LANES), F32)],
        compiler_params=_cparams("arbitrary"),
        name="route_rank",
    )(idx)


def _dispatch_kernel(pos_ref, x_ref, xs_in_ref, xs_ref, sem, *, bm):
    del xs_in_ref
    base = pl.program_id(0) * (bm * TOP_K)

    def copy(p):
        r = p // TOP_K
        return pltpu.make_async_copy(x_ref.at[pl.ds(r, 1), :], xs_ref.at[pl.ds(pos_ref[base + p], 1), :], sem)

    def start(p, carry):
        copy(p).start()
        return carry

    def wait(p, carry):
        copy(p).wait()
        return carry

    lax.fori_loop(0, bm * TOP_K, start, 0)
    lax.fori_loop(0, bm * TOP_K, wait, 0)


def _dispatch(x, pos_flat, n_rows, bm):
    t, d = x.shape
    xs0 = jnp.zeros((n_rows, d), x.dtype)
    return pl.pallas_call(
        functools.partial(_dispatch_kernel, bm=bm),
        grid_spec=pltpu.PrefetchScalarGridSpec(
            num_scalar_prefetch=1,
            grid=(t // bm,),
            in_specs=[pl.BlockSpec((bm, d), lambda i, pos: (i, 0)), pl.BlockSpec(memory_space=pl.ANY)],
            out_specs=pl.BlockSpec(memory_space=pl.ANY),
            scratch_shapes=[pltpu.SemaphoreType.DMA(())],
        ),
        out_shape=jax.ShapeDtypeStruct((n_rows, d), x.dtype),
        input_output_aliases={2: 0},
        compiler_params=_cparams("arbitrary"),
        name="moe_dispatch",
    )(pos_flat, x, xs0)


def _expert_mlp_kernel(te_ref, nv_ref, xs_ref, w1g_ref, w1l_ref, b1g_ref, b1l_ref, w2_ref, b2_ref, ys_ref, xb_ref):
    i = pl.program_id(0)
    j = pl.program_id(1)

    @pl.when(i < nv_ref[0])
    def _():
        @pl.when(j == 0)
        def _():
            xb_ref[...] = xs_ref[...].astype(BF16)
            ys_ref[...] = jnp.broadcast_to(b2_ref[0], ys_ref.shape)

        xb = xb_ref[...]
        ug = jnp.dot(xb, w1g_ref[0].astype(BF16), preferred_element_type=F32) + b1g_ref[0]
        ul = jnp.dot(xb, w1l_ref[0].astype(BF16), preferred_element_type=F32) + b1l_ref[0]
        x_glu = jnp.minimum(ug, SWIGLU_LIMIT)
        x_lin = jnp.clip(ul, -SWIGLU_LIMIT, SWIGLU_LIMIT)
        act = x_glu * jax.nn.sigmoid(SWIGLU_ALPHA * x_glu) * (x_lin + 1.0)
        ys_ref[...] += jnp.dot(act.astype(BF16), w2_ref[0].astype(BF16), preferred_element_type=F32)

    @pl.when(jnp.logical_and(i >= nv_ref[0], j == 0))
    def _():
        ys_ref[...] = jnp.zeros_like(ys_ref)


def _expert_mlp(xs, tile_expert, n_valid, w1, b1, w2, b2, tm, th):
    n_rows, d = xs.shape
    n_exp, _, two_h = w1.shape
    hdim = two_h // 2
    nj = hdim // th
    n_tiles = n_rows // tm
    b1r = b1.reshape(n_exp, 1, two_h)
    b2r = b2.reshape(n_exp, 1, d)

    def jj(i, j, nv):
        return jnp.where(i < nv[0], j, nj - 1)

    def row(i, j, te, nv):
        return (jnp.minimum(i, nv[0] - 1), 0)

    return pl.pallas_call(
        _expert_mlp_kernel,
        grid_spec=pltpu.PrefetchScalarGridSpec(
            num_scalar_prefetch=2,
            grid=(n_tiles, nj),
            in_specs=[
                pl.BlockSpec((tm, d), row),
                pl.BlockSpec((1, d, th), lambda i, j, te, nv: (te[i], 0, jj(i, j, nv))),
                pl.BlockSpec((1, d, th), lambda i, j, te, nv: (te[i], 0, nj + jj(i, j, nv))),
                pl.BlockSpec((1, 1, th), lambda i, j, te, nv: (te[i], 0, jj(i, j, nv))),
                pl.BlockSpec((1, 1, th), lambda i, j, te, nv: (te[i], 0, nj + jj(i, j, nv))),
                pl.BlockSpec((1, th, d), lambda i, j, te, nv: (te[i], jj(i, j, nv), 0)),
                pl.BlockSpec((1, 1, d), lambda i, j, te, nv: (te[i], 0, 0)),
            ],
            out_specs=pl.BlockSpec((tm, d), lambda i, j, te, nv: (i, 0)),
            scratch_shapes=[pltpu.VMEM((tm, d), BF16)],
        ),
        out_shape=jax.ShapeDtypeStruct((n_rows, d), F32),
        compiler_params=_cparams("arbitrary", "arbitrary"),
        name="expert_mlp",
    )(tile_expert, n_valid, xs, w1, w1, b1r, b1r, w2, b2r)


def _combine_ln_kernel(pos_ref, ys_ref, gate_ref, x_ref, g_ref, b_ref, o_ref, ob_ref, buf, sem, *, bm, alpha):
    base = pl.program_id(0) * (bm * TOP_K)

    def copy(p):
        k = p % TOP_K
        r = p // TOP_K
        return pltpu.make_async_copy(ys_ref.at[pl.ds(pos_ref[base + p], 1), :], buf.at[k, pl.ds(r, 1), :], sem)

    def start(p, carry):
        copy(p).start()
        return carry

    def wait(p, carry):
        copy(p).wait()
        return carry

    lax.fori_loop(0, bm * TOP_K, start, 0)
    lax.fori_loop(0, bm * TOP_K, wait, 0)
    gates = gate_ref[...]
    acc = gates[:, 0:1] * buf[0]
    for k in range(1, TOP_K):
        acc = acc + gates[:, k:k + 1] * buf[k]
    y = _layer_norm_rows(alpha * x_ref[...] + acc, g_ref[...], b_ref[...])
    o_ref[...] = y
    ob_ref[...] = y.astype(BF16)


def _combine_ln(ys, pos_flat, gates, x, ln_g, ln_b, alpha, bm):
    t, d = x.shape
    row = lambda i, pos: (i, 0)
    const = lambda i, pos: (0, 0)
    return pl.pallas_call(
        functools.partial(_combine_ln_kernel, bm=bm, alpha=alpha),
        grid_spec=pltpu.PrefetchScalarGridSpec(
            num_scalar_prefetch=1,
            grid=(t // bm,),
            in_specs=[
                pl.BlockSpec(memory_space=pl.ANY),
                pl.BlockSpec((bm, LANES), row),
                pl.BlockSpec((bm, d), row),
                pl.BlockSpec((1, d), const),
                pl.BlockSpec((1, d), const),
            ],
            out_specs=[pl.BlockSpec((bm, d), row), pl.BlockSpec((bm, d), row)],
            scratch_shapes=[pltpu.VMEM((TOP_K, bm, d), F32), pltpu.SemaphoreType.DMA(())],
        ),
        out_shape=[jax.ShapeDtypeStruct((t, d), F32), jax.ShapeDtypeStruct((t, d), BF16)],
        compiler_params=_cparams("arbitrary"),
        name="moe_combine_ln",
    )(pos_flat, ys, gates, x, ln_g, ln_b)


def _moe_ln(x1, idx, gates, w1, b1, w2, b2, ln_g, ln_b, alpha, tm, th, bm_rank, bm_move):
    t, d = x1.shape
    n_exp = w1.shape[0]
    rank, counts = _rank(idx, bm_rank)
    counts = counts[0, :n_exp]
    padded = ((counts + tm - 1) // tm) * tm
    ends = jnp.cumsum(padded)
    offs = ends - padded
    top = idx[:, :TOP_K]
    pos = (offs[top] + rank[:, :TOP_K]).reshape(-1).astype(jnp.int32)
    n_tiles = (t * TOP_K) // tm + n_exp
    tile_expert = jnp.minimum(jnp.searchsorted(ends, jnp.arange(n_tiles, dtype=jnp.int32) * tm, side="right"),
                              n_exp - 1).astype(jnp.int32)
    n_valid = (ends[-1:] // tm).astype(jnp.int32)
    tile_expert = jnp.where(jnp.arange(n_tiles) < n_valid[0], tile_expert, tile_expert[jnp.maximum(n_valid[0] - 1, 0)])
    xs = _dispatch(x1, pos, n_tiles * tm, bm_move)
    ys = _expert_mlp(xs, tile_expert, n_valid, w1, b1, w2, b2, tm, th)
    return _combine_ln(ys, pos, gates, x1, ln_g, ln_b, alpha, bm_move)


def _pad_lanes(v, offset=0, fill=0.0):
    out = jnp.full((1, LANES), fill, F32)
    return out.at[0, offset:offset + v.shape[0]].set(v.astype(F32))


def _tiles(t):
    def pick(cands):
        for c in cands:
            if t % c == 0:
                return c
        raise ValueError(t)
    return dict(bm_proj=pick((1024, 512, 256, 64)), bm_ln=pick((256, 64)), bm_rank=pick((512, 256, 64)),
                bm_move=pick((256, 64)))


def kernel(x_prompt, x_sample, state_pool, state_conv, state_delta, w_in, conv_w, pool_w, pool_scale, a_log, dt_bias,
           dn_norm_w, w_out, ln1_g, ln1_b, w_router, b_router, w1, b1, w2, b2, ln2_g, ln2_b):
    bp, lp, d = x_prompt.shape
    bs, ls, _ = x_sample.shape
    depth = w_in.shape[0]
    n_exp = w_router.shape[-1]
    alpha = float((2 * depth) ** 0.25)
    tp, ts = bp * lp, bs * ls
    t = tp + ts
    tl = _tiles(t)
    c_p = CHUNK if lp % CHUNK == 0 else lp
    c_s = CHUNK if ls % CHUNK == 0 else ls
    tm = 512 if (t * TOP_K) % 512 == 0 and t >= 4096 else 64
    th = 256

    x = jnp.concatenate([x_prompt.reshape(tp, d), x_sample.reshape(ts, d)], axis=0)
    xb = x.astype(BF16)
    zero_pool = jnp.zeros((bp, POOL_PAD, POOL_WIDTH), F32)
    zero_conv = jnp.zeros((bp, CONV_PAD, QKV_WIDTH), F32)
    zero_state = jnp.zeros((bp, N_HEADS, HEAD_DIM, HEAD_DIM), F32)
    outs = {k: [] for k in ("pp", "pc", "pd", "sp", "sc", "sd")}
    for l in range(depth):
        w_main = w_in[l][:, :MAIN_WIDTH].astype(BF16)
        w_small = jnp.pad(w_in[l][:, MAIN_WIDTH:], ((0, 0), (0, LANES - 2 * N_HEADS))).astype(BF16)
        z = _matmul(xb, w_main, tl["bm_proj"], 512)
        zs = _matmul(xb, w_small, tl["bm_proj"], LANES)
        a_log_row = _pad_lanes(a_log[l], N_HEADS)
        dt_row = _pad_lanes(dt_bias[l], N_HEADS)
        seq_w = (conv_w[l], pool_w[l].astype(BF16), pool_scale[l].reshape(1, -1), a_log_row, dt_row,
                 dn_norm_w[l].reshape(1, -1))
        hp = jnp.pad(state_pool[l], ((0, 0), (POOL_PAD - POOL_HIST, 0), (0, 0)))
        hc = jnp.pad(state_conv[l], ((0, 0), (CONV_PAD - (CONV_W - 1), 0), (0, 0)))
        mix_p, sd_p = _seq_mixer(z, zs, 0, bp, lp, c_p, 0, zero_pool, zero_conv, zero_state, *seq_w)
        mix_s, sd_s = _seq_mixer(z, zs, tp, bs, ls, c_s, PAST_LEN, hp, hc, state_delta[l], *seq_w)
        mix = jnp.concatenate([mix_p, mix_s], axis=0)
        zp = z[:tp].reshape(bp, lp, MAIN_WIDTH)
        zsm = z[tp:].reshape(bs, ls, MAIN_WIDTH)
        outs["pp"].append(zp[:, lp - POOL_HIST:, :POOL_WIDTH])
        outs["pc"].append(zp[:, lp - (CONV_W - 1):, POOL_WIDTH:GATE_COL])
        outs["pd"].append(sd_p)
        outs["sp"].append(zsm[:, ls - POOL_HIST:, :POOL_WIDTH])
        outs["sc"].append(zsm[:, ls - (CONV_W - 1):, POOL_WIDTH:GATE_COL])
        outs["sd"].append(sd_s)
        wr = jnp.pad(w_router[l], ((0, 0), (0, LANES - n_exp)))
        br = _pad_lanes(b_router[l])
        x1, idx, gates = _outproj_ln_router(mix, w_out[l].astype(BF16), x, ln1_g[l].reshape(1, -1),
                                            ln1_b[l].reshape(1, -1), wr, br, alpha, n_exp, tl["bm_ln"])
        x, xb = _moe_ln(x1, idx, gates, w1[l], b1[l], w2[l], b2[l], ln2_g[l].reshape(1, -1), ln2_b[l].reshape(1, -1),
                        alpha, tm, th, tl["bm_rank"], tl["bm_move"])
    y_prompt = x[:tp].reshape(bp, lp, d)
    y_sample = x[tp:].reshape(bs, ls, d)
    st = lambda k: jnp.stack(outs[k])
    return (y_prompt, y_sample, st("pp"), st("pc"), st("pd"), st("sp"), st("sc"), st("sd"))
```

```python
import functools

import jax
import jax.numpy as jnp
from jax import lax
from jax.experimental import pallas as pl
from jax.experimental.pallas import tpu as pltpu

F32 = jnp.float32
BF16 = jnp.bfloat16
U32 = jnp.uint32
HIGHEST = lax.Precision.HIGHEST

POOL_WINDOWS = (2, 4, 8, 16)
POOL_GROUP = 128
POOL_WIDTH = POOL_GROUP * len(POOL_WINDOWS)
POOL_HIST = max(POOL_WINDOWS) - 1
HEAD_DIM = 128
N_HEADS = 12
DN_WIDTH = N_HEADS * HEAD_DIM
QKV_WIDTH = 3 * DN_WIDTH
CONV_W = 4
MAIN_WIDTH = POOL_WIDTH + QKV_WIDTH + DN_WIDTH
GATE_COL = POOL_WIDTH + QKV_WIDTH
CHUNK = 64
PAST_LEN = 4096
TOP_K = 4
SWIGLU_LIMIT = 7.0
SWIGLU_ALPHA = 1.702
LN_EPS = 1e-5
RMS_EPS = 1e-6
L2_EPS = 1e-6

LANES = 128
SUBLANES = 8
SOLVE_BLOCK = 16
HEAD_GROUP = 4
POOL_PAD = 16
CONV_PAD = 8
VMEM_LIMIT = 56 * 1024 * 1024
HI_MASK = 0xFFFF0000


def _cparams(*sem):
    return pltpu.CompilerParams(dimension_semantics=sem, vmem_limit_bytes=VMEM_LIMIT)


def _dotb(a, b):
    return jnp.dot(a.astype(BF16), b.astype(BF16), preferred_element_type=F32)


def _matmul_kernel(x_ref, w_ref, o_ref):
    o_ref[...] = jnp.dot(x_ref[...], w_ref[...], preferred_element_type=F32).astype(o_ref.dtype)


def _matmul(x, w, bm, bn, out_dtype=F32):
    m, k = x.shape
    n = w.shape[1]
    return pl.pallas_call(
        _matmul_kernel,
        grid=(m // bm, n // bn),
        in_specs=[pl.BlockSpec((bm, k), lambda i, j: (i, 0)), pl.BlockSpec((k, bn), lambda i, j: (0, j))],
        out_specs=pl.BlockSpec((bm, bn), lambda i, j: (i, j)),
        out_shape=jax.ShapeDtypeStruct((m, n), out_dtype),
        compiler_params=_cparams("parallel", "arbitrary"),
        name="in_proj",
    )(x, w)


def _unit_lower_inverses(ls, n, nb):
    r = lax.broadcasted_iota(jnp.int32, (n, n), 0)
    q = lax.broadcasted_iota(jnp.int32, (n, n), 1)
    eye = (r == q).astype(F32)
    same = (r // SOLVE_BLOCK) == (q // SOLVE_BLOCK)
    p = [jnp.where(same, l, 0.0) for l in ls]
    e = [jnp.where(same, 0.0, l) for l in ls]
    t_d = [eye - x for x in p]
    for _ in range(SOLVE_BLOCK.bit_length() - 2):
        p = [_dotb(x, x) for x in p]
        t_d = [t + _dotb(t, x) for t, x in zip(t_d, p)]
    if nb == 1:
        return t_d
    nn = [_dotb(t, x) for t, x in zip(t_d, e)]
    if nb == 2:
        return [t - _dotb(x, t) for t, x in zip(t_d, nn)]
    assert nb == 4, nb
    n2 = [_dotb(x, x) for x in nn]
    m1 = [t + _dotb(x, t) for t, x in zip(t_d, n2)]
    return [m - _dotb(x, m) for m, x in zip(m1, nn)]


def _seq_mixer_kernel(z_ref, zs_ref, hp_ref, hc_ref, s0_ref, convw_ref, poolw_ref, pscale_ref, alog_ref, dtb_ref,
                      nw_ref, mix_ref, s_ref, pstate_ref, cstate_ref, pext, cext, *, c, pos0):
    ci = pl.program_id(1)
    gh = HEAD_GROUP * c
    groups = range(N_HEADS // HEAD_GROUP)

    @pl.when(ci == 0)
    def _():
        pext[0:POOL_PAD, :] = hp_ref[0]
        cext[0:CONV_PAD, :] = hc_ref[0]
        s_ref[...] = s0_ref[...]

    pext[POOL_PAD:POOL_PAD + c, :] = z_ref[:, 0:POOL_WIDTH]
    cext[CONV_PAD:CONV_PAD + c, :] = z_ref[:, POOL_WIDTH:GATE_COL]

    pos = lax.broadcasted_iota(jnp.int32, (c, 1), 0) + ci * c + pos0
    for gi, w in enumerate(POOL_WINDOWS):
        cols = slice(gi * POOL_GROUP, (gi + 1) * POOL_GROUP)
        u = pext[POOL_PAD:POOL_PAD + c, cols]
        s = u
        for j in range(1, w):
            s = s + pext[POOL_PAD - j:POOL_PAD - j + c, cols]
        cnt = jnp.minimum(pos + 1, w).astype(F32)
        d = s / cnt - u
        y = jnp.dot(d.astype(BF16), poolw_ref[gi], preferred_element_type=F32)
        mix_ref[:, cols] = (y * pscale_ref[:, cols]).astype(mix_ref.dtype)

    zs = zs_ref[...]
    beta_all = jax.nn.sigmoid(zs)
    a = zs + dtb_ref[...]
    softplus = jnp.maximum(a, 0.0) + jnp.log1p(jnp.exp(-jnp.abs(a)))
    g_all = -jnp.exp(alog_ref[...]) * softplus
    rc = lax.broadcasted_iota(jnp.int32, (c, c), 0)
    qc = lax.broadcasted_iota(jnp.int32, (c, c), 1)
    gc_all = jnp.dot((rc >= qc).astype(F32), g_all, preferred_element_type=F32, precision=HIGHEST)

    def conv_block(col0):
        cols = slice(col0, col0 + HEAD_DIM)
        acc = convw_ref[0:1, cols] * cext[CONV_PAD - (CONV_W - 1):CONV_PAD - (CONV_W - 1) + c, cols]
        for j in range(1, CONV_W):
            lo = CONV_PAD - (CONV_W - 1) + j
            acc = acc + convw_ref[j:j + 1, cols] * cext[lo:lo + c, cols]
        return acc * jax.nn.sigmoid(acc)

    r2 = lax.broadcasted_iota(jnp.int32, (gh, gh), 0)
    q2 = lax.broadcasted_iota(jnp.int32, (gh, gh), 1)
    same_head = (r2 // c) == (q2 // c)
    incl = jnp.logical_and(same_head, r2 >= q2)
    strict = jnp.logical_and(same_head, r2 > q2)
    row_head = lax.broadcasted_iota(jnp.int32, (gh, LANES), 0) // c
    lane = lax.broadcasted_iota(jnp.int32, (gh, LANES), 1)
    gc_stack = jnp.concatenate([gc_all] * HEAD_GROUP, axis=0)
    beta_stack = jnp.concatenate([beta_all] * HEAD_GROUP, axis=0)
    ones_rows = jnp.ones((SUBLANES, LANES), F32)
    nt = (((1,), (1,)), ((), ()))
    tn = (((0,), (0,)), ((), ()))

    def stack(fn, g):
        return jnp.concatenate([fn(g * HEAD_GROUP + hh) for hh in range(HEAD_GROUP)], axis=0)

    def l2n(x):
        return x * lax.rsqrt(jnp.sum(x * x, -1, keepdims=True) + L2_EPS)

    q4 = [stack(lambda h: l2n(conv_block(h * HEAD_DIM)) * (HEAD_DIM ** -0.5), g) for g in groups]
    k4 = [stack(lambda h: l2n(conv_block(DN_WIDTH + h * HEAD_DIM)), g) for g in groups]
    v4 = [stack(lambda h: conv_block(2 * DN_WIDTH + h * HEAD_DIM), g) for g in groups]
    gsel = [jnp.where(lane == N_HEADS + g * HEAD_GROUP + row_head, gc_stack, 0.0) for g in groups]
    gcol = [jnp.sum(x, -1, keepdims=True) for x in gsel]
    grow = [lax.dot_general(ones_rows, x, nt, preferred_element_type=F32, precision=HIGHEST)[0:1, :] for x in gsel]
    beta = [jnp.sum(jnp.where(lane == g * HEAD_GROUP + row_head, beta_stack, 0.0), -1, keepdims=True) for g in groups]
    kb = [x.astype(BF16) for x in k4]
    qb = [x.astype(BF16) for x in q4]
    kk = [lax.dot_general(x, x, nt, preferred_element_type=F32) for x in kb]
    qk = [lax.dot_general(x, y, nt, preferred_element_type=F32) for x, y in zip(qb, kb)]
    dec = [jnp.exp(jnp.where(incl, gc - gr, -jnp.inf)) for gc, gr in zip(gcol, grow)]
    l_mat = [b * x * jnp.where(strict, d, 0.0) for b, x, d in zip(beta, kk, dec)]
    t_inv = _unit_lower_inverses(l_mat, gh, c // SOLVE_BLOCK)
    eg = [jnp.exp(x) for x in gcol]
    uw = [_dotb(t, jnp.concatenate([b * v, (b * e) * k], axis=1))
          for t, b, v, e, k in zip(t_inv, beta, v4, eg, k4)]
    rows = lambda x, hh: x[hh * c:(hh + 1) * c]
    s_old = [s_ref[0, h] for h in range(N_HEADS)]
    ws_qs = [[_dotb(jnp.concatenate([rows(uw[g], hh)[:, HEAD_DIM:], rows(q4[g], hh)], axis=0),
                    s_old[g * HEAD_GROUP + hh]) for hh in range(HEAD_GROUP)] for g in groups]
    u4 = [uw[g][:, :HEAD_DIM] - jnp.concatenate([x[:c] for x in ws_qs[g]], axis=0) for g in groups]
    qs4 = [jnp.concatenate([x[c:] for x in ws_qs[g]], axis=0) for g in groups]
    ub = [x.astype(BF16) for x in u4]
    o4 = [e * qs + jnp.dot((x * d).astype(BF16), u, preferred_element_type=F32)
          for e, qs, x, d, u in zip(eg, qs4, qk, dec, ub)]
    for g in groups:
        for hh in range(HEAD_GROUP):
            h = g * HEAD_GROUP + hh
            g_last = gc_all[c - 1:c, N_HEADS + h:N_HEADS + h + 1]
            tail = jnp.exp(g_last - rows(gcol[g], hh))
            s_ref[0, h] = jnp.exp(g_last) * s_old[h] + lax.dot_general(
                (rows(k4[g], hh) * tail).astype(BF16), rows(ub[g], hh), tn, preferred_element_type=F32)
            o = rows(o4[g], hh)
            o = o * lax.rsqrt(jnp.mean(o * o, -1, keepdims=True) + RMS_EPS) * nw_ref[...]
            gate = z_ref[:, GATE_COL + h * HEAD_DIM:GATE_COL + (h + 1) * HEAD_DIM]
            o = o * (gate * jax.nn.sigmoid(gate))
            mix_ref[:, POOL_WIDTH + h * HEAD_DIM:POOL_WIDTH + (h + 1) * HEAD_DIM] = o.astype(mix_ref.dtype)

    p_tail = pext[c:c + POOL_PAD, :]
    c_tail = cext[c:c + CONV_PAD, :]
    pext[0:POOL_PAD, :] = p_tail
    cext[0:CONV_PAD, :] = c_tail
    pstate_ref[0] = p_tail
    cstate_ref[0] = c_tail


def _seq_mixer(z, zs, row0, nseq, seqlen, c, pos0, hist_pool, hist_conv, s0, conv_w, pool_w, pool_scale, a_log_row,
               dt_bias_row, norm_w):
    nchunk = seqlen // c
    blk0 = row0 // c
    d_mix = POOL_WIDTH + DN_WIDTH
    row_map = lambda b, ci: (blk0 + b * nchunk + ci, 0)
    seq_map3 = lambda b, ci: (b, 0, 0)
    seq_map4 = lambda b, ci: (b, 0, 0, 0)
    const2 = lambda b, ci: (0, 0)
    kern = functools.partial(_seq_mixer_kernel, c=c, pos0=pos0)
    return pl.pallas_call(
        kern,
        grid=(nseq, nchunk),
        in_specs=[
            pl.BlockSpec((c, MAIN_WIDTH), row_map),
            pl.BlockSpec((c, LANES), row_map),
            pl.BlockSpec((1, POOL_PAD, POOL_WIDTH), seq_map3),
            pl.BlockSpec((1, CONV_PAD, QKV_WIDTH), seq_map3),
            pl.BlockSpec((1, N_HEADS, HEAD_DIM, HEAD_DIM), seq_map4),
            pl.BlockSpec((CONV_W, QKV_WIDTH), const2),
            pl.BlockSpec((len(POOL_WINDOWS), POOL_GROUP, POOL_GROUP), lambda b, ci: (0, 0, 0)),
            pl.BlockSpec((1, POOL_WIDTH), const2),
            pl.BlockSpec((1, LANES), const2),
            pl.BlockSpec((1, LANES), const2),
            pl.BlockSpec((1, HEAD_DIM), const2),
        ],
        out_specs=[
            pl.BlockSpec((c, d_mix), lambda b, ci: (b * nchunk + ci, 0)),
            pl.BlockSpec((1, N_HEADS, HEAD_DIM, HEAD_DIM), seq_map4),
            pl.BlockSpec((1, POOL_PAD, POOL_WIDTH), seq_map3),
            pl.BlockSpec((1, CONV_PAD, QKV_WIDTH), seq_map3),
        ],
        out_shape=[
            jax.ShapeDtypeStruct((nseq * seqlen, d_mix), BF16),
            jax.ShapeDtypeStruct((nseq, N_HEADS, HEAD_DIM, HEAD_DIM), F32),
            jax.ShapeDtypeStruct((nseq, POOL_PAD, POOL_WIDTH), F32),
            jax.ShapeDtypeStruct((nseq, CONV_PAD, QKV_WIDTH), F32),
        ],
        scratch_shapes=[pltpu.VMEM((POOL_PAD + c, POOL_WIDTH), F32), pltpu.VMEM((CONV_PAD + c, QKV_WIDTH), F32)],
        compiler_params=_cparams("parallel", "arbitrary"),
        name="seq_mixer",
    )(z, zs, hist_pool, hist_conv, s0, conv_w, pool_w, pool_scale, a_log_row, dt_bias_row, norm_w)


def _layer_norm_rows(x, g, b):
    xc = x - jnp.mean(x, -1, keepdims=True)
    var = jnp.mean(xc * xc, -1, keepdims=True)
    return xc * lax.rsqrt(var + LN_EPS) * g + b


def _pack_bf16_pairs(x):
    half = x.shape[1] // 2
    lo = lax.bitcast_convert_type(x[:, :half].astype(BF16).astype(F32), U32) >> 16
    hi = lax.bitcast_convert_type(x[:, half:].astype(BF16).astype(F32), U32) & jnp.uint32(HI_MASK)
    return lo | hi


def _outproj_ln_router_kernel(mix_ref, wout_ref, x_ref, g_ref, b_ref, wr_ref, br_ref, x1_ref, xp_ref, idx_ref,
                              gate_ref, *, alpha, n_experts):
    m = jnp.dot(mix_ref[...], wout_ref[...], preferred_element_type=F32)
    x1 = _layer_norm_rows(alpha * x_ref[...] + m, g_ref[...], b_ref[...])
    x1_ref[...] = x1
    words = _pack_bf16_pairs(x1)
    bm = x1.shape[0]
    sp = xp_ref.shape[0] // bm
    for s in range(sp):
        xp_ref[pl.ds(s, bm, stride=sp), :] = words[:, s * LANES:(s + 1) * LANES]
    logits = jnp.dot(x1, wr_ref[...], preferred_element_type=F32, precision=HIGHEST) + br_ref[...]
    lane = lax.broadcasted_iota(jnp.int32, logits.shape, 1).astype(F32)
    logits = jnp.where(lane < n_experts, logits, -jnp.inf)
    vals, idxs = [], []
    for _ in range(TOP_K):
        v = jnp.max(logits, -1, keepdims=True)
        i = jnp.min(jnp.where(logits == v, lane, float(LANES)), -1, keepdims=True)
        vals.append(v)
        idxs.append(i)
        logits = jnp.where(lane == i, -jnp.inf, logits)
    es = [jnp.exp(v - vals[0]) for v in vals]
    denom = es[0]
    for e in es[1:]:
        denom = denom + e
    idx_out = jnp.zeros(logits.shape, F32)
    gate_out = jnp.zeros(logits.shape, F32)
    for k in range(TOP_K):
        idx_out = jnp.where(lane == k, idxs[k], idx_out)
        gate_out = jnp.where(lane == k, es[k] / denom, gate_out)
    idx_ref[...] = idx_out.astype(jnp.int32)
    gate_ref[...] = gate_out


def _outproj_ln_router(mix, w_out, x, ln_g, ln_b, w_router_pad, b_router_pad, alpha, n_experts, bm):
    t, d = x.shape
    sp = d // (2 * LANES)
    row = lambda i: (i, 0)
    const = lambda i: (0, 0)
    kern = functools.partial(_outproj_ln_router_kernel, alpha=alpha, n_experts=n_experts)
    return pl.pallas_call(
        kern,
        grid=(t // bm,),
        in_specs=[
            pl.BlockSpec((bm, mix.shape[1]), row),
            pl.BlockSpec(w_out.shape, const),
            pl.BlockSpec((bm, d), row),
            pl.BlockSpec((1, d), const),
            pl.BlockSpec((1, d), const),
            pl.BlockSpec((d, LANES), const),
            pl.BlockSpec((1, LANES), const),
        ],
        out_specs=[pl.BlockSpec((bm, d), row), pl.BlockSpec((bm * sp, LANES), row),
                   pl.BlockSpec((bm, LANES), row), pl.BlockSpec((bm, LANES), row)],
        out_shape=[
            jax.ShapeDtypeStruct((t, d), F32),
            jax.ShapeDtypeStruct((t * sp, LANES), U32),
            jax.ShapeDtypeStruct((t, LANES), jnp.int32),
            jax.ShapeDtypeStruct((t, LANES), F32),
        ],
        compiler_params=_cparams("parallel"),
        name="outproj_ln_router",
    )(mix, w_out, x, ln_g, ln_b, w_router_pad, b_router_pad)


def _rank_kernel(idx_ref, rank_ref, count_ref, carry):
    @pl.when(pl.program_id(0) == 0)
    def _():
        carry[...] = jnp.zeros_like(carry)

    idx = idx_ref[...]
    bm = idx.shape[0]
    lane = lax.broadcasted_iota(jnp.int32, (bm, LANES), 1)
    onehots = [(idx[:, k:k + 1] == lane) for k in range(TOP_K)]
    cnt = onehots[0].astype(F32)
    for oh in onehots[1:]:
        cnt = cnt + oh.astype(F32)
    r = lax.broadcasted_iota(jnp.int32, (bm, bm), 0)
    q = lax.broadcasted_iota(jnp.int32, (bm, bm), 1)
    before = jnp.dot((r > q).astype(BF16), cnt.astype(BF16), preferred_element_type=F32) + carry[...]
    out = jnp.zeros((bm, LANES), jnp.int32)
    for k in range(TOP_K):
        rk = jnp.sum(jnp.where(onehots[k], before, 0.0), -1, keepdims=True).astype(jnp.int32)
        out = jnp.where(lane == k, rk, out)
    rank_ref[...] = out
    total = carry[...] + jnp.sum(cnt, 0, keepdims=True)
    carry[...] = total
    count_ref[...] = total.astype(jnp.int32)


def _rank(idx, bm):
    t = idx.shape[0]
    return pl.pallas_call(
        _rank_kernel,
        grid=(t // bm,),
        in_specs=[pl.BlockSpec((bm, LANES), lambda i: (i, 0))],
        out_specs=[pl.BlockSpec((bm, LANES), lambda i: (i, 0)), pl.BlockSpec((1, LANES), lambda i: (0, 0))],
        out_shape=[jax.ShapeDtypeStruct((t, LANES), jnp.int32), jax.ShapeDtypeStruct((1, LANES), jnp.int32)],
        scratch_shapes=[pltpu.VMEM((1, LANES), F32)],
        compiler_params=_cparams("arbitrary"),
        name="route_rank",
    )(idx)


def _token_rows(r, sp):
    return pl.ds(pl.multiple_of(r * sp, sp), sp)


def _dispatch_kernel(pos_ref, x_ref, xs_in_ref, xs_ref, sem, *, bm, sp):
    del xs_in_ref
    base = pl.program_id(0) * (bm * TOP_K)

    def copy(p):
        return pltpu.make_async_copy(x_ref.at[_token_rows(p // TOP_K, sp), :],
                                     xs_ref.at[_token_rows(pos_ref[base + p], sp), :], sem)

    def start(p, carry):
        copy(p).start()
        return carry

    def wait(p, carry):
        copy(p).wait()
        return carry

    lax.fori_loop(0, bm * TOP_K, start, 0)
    lax.fori_loop(0, bm * TOP_K, wait, 0)


def _dispatch(xp, pos_flat, t, n_rows, bm):
    sp = xp.shape[0] // t
    xs0 = jnp.zeros((n_rows * sp, LANES), xp.dtype)
    return pl.pallas_call(
        functools.partial(_dispatch_kernel, bm=bm, sp=sp),
        grid_spec=pltpu.PrefetchScalarGridSpec(
            num_scalar_prefetch=1,
            grid=(t // bm,),
            in_specs=[pl.BlockSpec((bm * sp, LANES), lambda i, pos: (i, 0)), pl.BlockSpec(memory_space=pl.ANY)],
            out_specs=pl.BlockSpec(memory_space=pl.ANY),
            scratch_shapes=[pltpu.SemaphoreType.DMA(())],
        ),
        out_shape=jax.ShapeDtypeStruct((n_rows * sp, LANES), xp.dtype),
        input_output_aliases={2: 0},
        compiler_params=_cparams("arbitrary"),
        name="moe_dispatch",
    )(pos_flat, xp, xs0)


def _expert_mlp_kernel(te_ref, nr_ref, xs_ref, w1g_ref, w1l_ref, b1g_ref, b1l_ref, w2_ref, b2_ref, ys_ref, xb_ref,
                       acc_ref, wg_ref, wl_ref, w2b_ref, *, tm, subs):
    i = pl.program_id(0)
    j = pl.program_id(1)
    nj = pl.num_programs(1)
    n_rows = nr_ref[i]
    d = acc_ref.shape[1]
    half = d // 2
    sp = half // LANES
    so = d // LANES

    @pl.when(n_rows > 0)
    def _():
        wg_ref[...] = w1g_ref[0].astype(BF16)
        wl_ref[...] = w1l_ref[0].astype(BF16)
        w2b_ref[...] = w2_ref[0].astype(BF16)

    for sub in range(subs):
        r0 = sub * tm
        active = n_rows > r0

        @pl.when(jnp.logical_and(active, j == 0))
        def _():
            for s in range(sp):
                w = xs_ref[pl.ds(r0 * sp + s, tm, stride=sp), :]
                xb_ref[r0:r0 + tm, s * LANES:(s + 1) * LANES] = lax.bitcast_convert_type(w << 16, F32).astype(BF16)
                xb_ref[r0:r0 + tm, half + s * LANES:half + (s + 1) * LANES] = lax.bitcast_convert_type(
                    w & jnp.uint32(HI_MASK), F32).astype(BF16)
            acc_ref[r0:r0 + tm, :] = jnp.broadcast_to(b2_ref[0], (tm, acc_ref.shape[1]))

        @pl.when(active)
        def _():
            xb = xb_ref[r0:r0 + tm, :]
            ug = jnp.dot(xb, wg_ref[...], preferred_element_type=F32) + b1g_ref[0]
            ul = jnp.dot(xb, wl_ref[...], preferred_element_type=F32) + b1l_ref[0]
            x_glu = jnp.minimum(ug, SWIGLU_LIMIT)
            x_lin = jnp.clip(ul, -SWIGLU_LIMIT, SWIGLU_LIMIT)
            act = x_glu * jax.nn.sigmoid(SWIGLU_ALPHA * x_glu) * (x_lin + 1.0)
            acc_ref[r0:r0 + tm, :] += jnp.dot(act.astype(BF16), w2b_ref[...], preferred_element_type=F32)

        @pl.when(jnp.logical_and(active, j == nj - 1))
        def _():
            for s in range(so):
                ys_ref[pl.ds(r0 * so + s, tm, stride=so), :] = acc_ref[r0:r0 + tm, s * LANES:(s + 1) * LANES]

        @pl.when(jnp.logical_and(jnp.logical_not(active), j == nj - 1))
        def _():
            ys_ref[r0 * so:(r0 + tm) * so, :] = jnp.zeros((tm * so, LANES), ys_ref.dtype)


def _expert_mlp(xs, tile_expert, tile_rows, w1, b1, w2, b2, tm, subs, th):
    n_we, d, two_h = w1.shape
    sp = d // (2 * LANES)
    so = d // LANES
    n_rows = xs.shape[0] // sp
    hdim = two_h // 2
    nj = hdim // th
    tb = tm * subs
    n_tiles = n_rows // tb
    b1r = b1.reshape(n_we, 1, two_h)
    b2r = b2.reshape(n_we, 1, d)

    def jj(i, j, nr):
        return jnp.where(nr[i] > 0, j, nj - 1)

    return pl.pallas_call(
        functools.partial(_expert_mlp_kernel, tm=tm, subs=subs),
        grid_spec=pltpu.PrefetchScalarGridSpec(
            num_scalar_prefetch=2,
            grid=(n_tiles, nj),
            in_specs=[
                pl.BlockSpec((tb * sp, LANES), lambda i, j, te, nr: (i, 0)),
                pl.BlockSpec((1, d, th), lambda i, j, te, nr: (te[i], 0, jj(i, j, nr))),
                pl.BlockSpec((1, d, th), lambda i, j, te, nr: (te[i], 0, nj + jj(i, j, nr))),
                pl.BlockSpec((1, 1, th), lambda i, j, te, nr: (te[i], 0, jj(i, j, nr))),
                pl.BlockSpec((1, 1, th), lambda i, j, te, nr: (te[i], 0, nj + jj(i, j, nr))),
                pl.BlockSpec((1, th, d), lambda i, j, te, nr: (te[i], jj(i, j, nr), 0)),
                pl.BlockSpec((1, 1, d), lambda i, j, te, nr: (te[i], 0, 0)),
            ],
            out_specs=pl.BlockSpec((tb * so, LANES), lambda i, j, te, nr: (i, 0)),
            scratch_shapes=[pltpu.VMEM((tb, d), BF16), pltpu.VMEM((tb, d), F32), pltpu.VMEM((d, th), BF16),
                            pltpu.VMEM((d, th), BF16), pltpu.VMEM((th, d), BF16)],
        ),
        out_shape=jax.ShapeDtypeStruct((n_rows * so, LANES), F32),
        compiler_params=_cparams("arbitrary", "arbitrary"),
        name="expert_mlp",
    )(tile_expert, tile_rows, xs, w1, w1, b1r, b1r, w2, b2r)


def _combine_ln_kernel(pos_ref, ys_ref, gate_ref, x_ref, g_ref, b_ref, o_ref, ob_ref, buf, acc_ref, sem, *, bm,
                       alpha):
    base = pl.program_id(0) * (bm * TOP_K)
    so = acc_ref.shape[1] // LANES

    def copy(p):
        return pltpu.make_async_copy(ys_ref.at[_token_rows(pos_ref[base + p], so), :],
                                     buf.at[p % TOP_K, _token_rows(p // TOP_K, so), :], sem)

    def start(p, carry):
        copy(p).start()
        return carry

    def wait(p, carry):
        copy(p).wait()
        return carry

    lax.fori_loop(0, bm * TOP_K, start, 0)
    lax.fori_loop(0, bm * TOP_K, wait, 0)
    gates = gate_ref[...]
    for s in range(so):
        acc = gates[:, 0:1] * buf[0, pl.ds(s, bm, stride=so), :]
        for k in range(1, TOP_K):
            acc = acc + gates[:, k:k + 1] * buf[k, pl.ds(s, bm, stride=so), :]
        acc_ref[:, s * LANES:(s + 1) * LANES] = acc
    y = _layer_norm_rows(alpha * x_ref[...] + acc_ref[...], g_ref[...], b_ref[...])
    o_ref[...] = y
    ob_ref[...] = y.astype(BF16)


def _combine_ln(ys, pos_flat, gates, x, ln_g, ln_b, alpha, bm):
    t, d = x.shape
    row = lambda i, pos: (i, 0)
    const = lambda i, pos: (0, 0)
    return pl.pallas_call(
        functools.partial(_combine_ln_kernel, bm=bm, alpha=alpha),
        grid_spec=pltpu.PrefetchScalarGridSpec(
            num_scalar_prefetch=1,
            grid=(t // bm,),
            in_specs=[
                pl.BlockSpec(memory_space=pl.ANY),
                pl.BlockSpec((bm, LANES), row),
                pl.BlockSpec((bm, d), row),
                pl.BlockSpec((1, d), const),
                pl.BlockSpec((1, d), const),
            ],
            out_specs=[pl.BlockSpec((bm, d), row), pl.BlockSpec((bm, d), row)],
            scratch_shapes=[pltpu.VMEM((TOP_K, bm * (d // LANES), LANES), F32), pltpu.VMEM((bm, d), F32),
                            pltpu.SemaphoreType.DMA(())],
        ),
        out_shape=[jax.ShapeDtypeStruct((t, d), F32), jax.ShapeDtypeStruct((t, d), BF16)],
        compiler_params=_cparams("arbitrary"),
        name="moe_combine_ln",
    )(pos_flat, ys, gates, x, ln_g, ln_b)


def _moe_ln(x1, xp, idx, gates, layer, n_exp, w1, b1, w2, b2, ln_g, ln_b, alpha, tm, subs, th, bm_rank, bm_move):
    t, d = x1.shape
    tb = tm * subs
    rank, counts = _rank(idx, bm_rank)
    counts = counts[0, :n_exp]
    padded = ((counts + tb - 1) // tb) * tb
    ends = jnp.cumsum(padded)
    offs = ends - padded
    top = idx[:, :TOP_K]
    pos = (offs[top] + rank[:, :TOP_K]).reshape(-1).astype(jnp.int32)
    n_tiles = -(-(t * TOP_K) // tb) + n_exp
    starts = jnp.arange(n_tiles, dtype=jnp.int32) * tb
    tile_expert = jnp.minimum(jnp.sum((ends[None, :] <= starts[:, None]).astype(jnp.int32), axis=1), n_exp - 1)
    tile_rows = jnp.clip(counts[tile_expert] - (starts - offs[tile_expert]), 0, tb)
    tile_rows = jnp.where(starts < ends[-1], tile_rows, 0).astype(jnp.int32)
    last_valid = jnp.maximum(ends[-1] // tb - 1, 0)
    tile_expert = jnp.where(starts < ends[-1], tile_expert, tile_expert[last_valid])
    tile_expert = (tile_expert + layer * n_exp).astype(jnp.int32)
    xs = _dispatch(xp, pos, t, n_tiles * tb, bm_move)
    ys = _expert_mlp(xs, tile_expert, tile_rows, w1, b1, w2, b2, tm, subs, th)
    return _combine_ln(ys, pos, gates, x1, ln_g, ln_b, alpha, bm_move)


def _pad_lanes(v, offset=0, fill=0.0):
    out = jnp.full((1, LANES), fill, F32)
    return out.at[0, offset:offset + v.shape[0]].set(v.astype(F32))


def _tiles(t):
    def pick(cands):
        for c in cands:
            if t % c == 0:
                return c
        raise ValueError(t)
    big = t >= 4096
    return dict(bm_proj=pick((1024, 512, 256, 64)), bm_ln=pick((256, 64)), bm_rank=pick((512, 256, 64)),
                bm_move=pick((256, 64)), tm=512 if big else 64, subs=2)


def kernel(x_prompt, x_sample, state_pool, state_conv, state_delta, w_in, conv_w, pool_w, pool_scale, a_log, dt_bias,
           dn_norm_w, w_out, ln1_g, ln1_b, w_router, b_router, w1, b1, w2, b2, ln2_g, ln2_b):
    bp, lp, d = x_prompt.shape
    bs, ls, _ = x_sample.shape
    depth = w_in.shape[0]
    n_exp = w_router.shape[-1]
    alpha = float((2 * depth) ** 0.25)
    tp, ts = bp * lp, bs * ls
    t = tp + ts
    tl = _tiles(t)
    c_p = CHUNK if lp % CHUNK == 0 else lp
    c_s = CHUNK if ls % CHUNK == 0 else ls
    th = 256
    assert lp >= POOL_HIST and ls >= POOL_HIST

    x = jnp.concatenate([x_prompt.reshape(tp, d), x_sample.reshape(ts, d)], axis=0)
    xb = x.astype(BF16)
    zero_pool = jnp.zeros((bp, POOL_PAD, POOL_WIDTH), F32)
    zero_conv = jnp.zeros((bp, CONV_PAD, QKV_WIDTH), F32)
    zero_state = jnp.zeros((bp, N_HEADS, HEAD_DIM, HEAD_DIM), F32)
    w1s = w1.reshape((depth * n_exp,) + w1.shape[2:])
    b1s = b1.reshape((depth * n_exp,) + b1.shape[2:])
    w2s = w2.reshape((depth * n_exp,) + w2.shape[2:])
    b2s = b2.reshape((depth * n_exp,) + b2.shape[2:])
    outs = {k: [] for k in ("pp", "pc", "pd", "sp", "sc", "sd")}
    for l in range(depth):
        w_main = w_in[l][:, :MAIN_WIDTH].astype(BF16)
        w_small = jnp.pad(w_in[l][:, MAIN_WIDTH:], ((0, 0), (0, LANES - 2 * N_HEADS))).astype(BF16)
        z = _matmul(xb, w_main, tl["bm_proj"], 512)
        zs = _matmul(xb, w_small, tl["bm_proj"], LANES)
        a_log_row = _pad_lanes(a_log[l], N_HEADS)
        dt_row = _pad_lanes(dt_bias[l], N_HEADS)
        seq_w = (conv_w[l], pool_w[l].astype(BF16), pool_scale[l].reshape(1, -1), a_log_row, dt_row,
                 dn_norm_w[l].reshape(1, -1))
        hp = jnp.pad(state_pool[l], ((0, 0), (POOL_PAD - POOL_HIST, 0), (0, 0)))
        hc = jnp.pad(state_conv[l], ((0, 0), (CONV_PAD - (CONV_W - 1), 0), (0, 0)))
        mix_p, sd_p, ps_p, cs_p = _seq_mixer(z, zs, 0, bp, lp, c_p, 0, zero_pool, zero_conv, zero_state, *seq_w)
        mix_s, sd_s, ps_s, cs_s = _seq_mixer(z, zs, tp, bs, ls, c_s, PAST_LEN, hp, hc, state_delta[l], *seq_w)
        mix = jnp.concatenate([mix_p, mix_s], axis=0)
        outs["pp"].append(ps_p[:, POOL_PAD - POOL_HIST:])
        outs["pc"].append(cs_p[:, CONV_PAD - (CONV_W - 1):])
        outs["pd"].append(sd_p)
        outs["sp"].append(ps_s[:, POOL_PAD - POOL_HIST:])
        outs["sc"].append(cs_s[:, CONV_PAD - (CONV_W - 1):])
        outs["sd"].append(sd_s)
        wr = jnp.pad(w_router[l], ((0, 0), (0, LANES - n_exp)))
        br = _pad_lanes(b_router[l])
        x1, xp, idx, gates = _outproj_ln_router(mix, w_out[l].astype(BF16), x, ln1_g[l].reshape(1, -1),
                                                ln1_b[l].reshape(1, -1), wr, br, alpha, n_exp, tl["bm_ln"])
        x, xb = _moe_ln(x1, xp, idx, gates, l, n_exp, w1s, b1s, w2s, b2s, ln2_g[l].reshape(1, -1),
                        ln2_b[l].reshape(1, -1), alpha, tl["tm"], tl["subs"], th, tl["bm_rank"], tl["bm_move"])
    y_prompt = x[:tp].reshape(bp, lp, d)
    y_sample = x[tp:].reshape(bs, ls, d)
    st = lambda k: jnp.stack(outs[k])
    return (y_prompt, y_sample, st("pp"), st("pc"), st("pd"), st("sp"), st("sc"), st("sd"))
```

```python
import functools

import jax
import jax.numpy as jnp
from jax import lax
from jax.experimental import pallas as pl
from jax.experimental.pallas import tpu as pltpu

F32 = jnp.float32
BF16 = jnp.bfloat16
U32 = jnp.uint32
HIGHEST = lax.Precision.HIGHEST

POOL_WINDOWS = (2, 4, 8, 16)
POOL_GROUP = 128
POOL_WIDTH = POOL_GROUP * len(POOL_WINDOWS)
POOL_HIST = max(POOL_WINDOWS) - 1
HEAD_DIM = 128
N_HEADS = 12
DN_WIDTH = N_HEADS * HEAD_DIM
QKV_WIDTH = 3 * DN_WIDTH
CONV_W = 4
MAIN_WIDTH = POOL_WIDTH + QKV_WIDTH + DN_WIDTH
GATE_COL = POOL_WIDTH + QKV_WIDTH
CHUNK = 64
PAST_LEN = 4096
TOP_K = 4
SWIGLU_LIMIT = 7.0
SWIGLU_ALPHA = 1.702
LN_EPS = 1e-5
RMS_EPS = 1e-6
L2_EPS = 1e-6

LANES = 128
SUBLANES = 8
SOLVE_BLOCK = 16
HEAD_GROUP = 4
POOL_PAD = 16
CONV_PAD = 8
VMEM_LIMIT = 56 * 1024 * 1024
HI_MASK = 0xFFFF0000
DMA_PRIORITIES = 2
COMBINE_PARTS = 2


def _cparams(*sem):
    return pltpu.CompilerParams(dimension_semantics=sem, vmem_limit_bytes=VMEM_LIMIT)


def _dotb(a, b):
    return jnp.dot(a.astype(BF16), b.astype(BF16), preferred_element_type=F32)


def _matmul_kernel(x_ref, w_ref, o_ref):
    o_ref[...] = jnp.dot(x_ref[...], w_ref[...], preferred_element_type=F32).astype(o_ref.dtype)


def _matmul(x, w, bm, bn, out_dtype=F32):
    m, k = x.shape
    n = w.shape[1]
    return pl.pallas_call(
        _matmul_kernel,
        grid=(m // bm, n // bn),
        in_specs=[pl.BlockSpec((bm, k), lambda i, j: (i, 0)), pl.BlockSpec((k, bn), lambda i, j: (0, j))],
        out_specs=pl.BlockSpec((bm, bn), lambda i, j: (i, j)),
        out_shape=jax.ShapeDtypeStruct((m, n), out_dtype),
        compiler_params=_cparams("parallel", "arbitrary"),
        name="in_proj",
    )(x, w)


def _unit_lower_inverses(ls, n, nb):
    r = lax.broadcasted_iota(jnp.int32, (n, n), 0)
    q = lax.broadcasted_iota(jnp.int32, (n, n), 1)
    eye = (r == q).astype(F32)
    same = (r // SOLVE_BLOCK) == (q // SOLVE_BLOCK)
    p = [jnp.where(same, l, 0.0) for l in ls]
    e = [jnp.where(same, 0.0, l) for l in ls]
    t_d = [eye - x for x in p]
    for _ in range(SOLVE_BLOCK.bit_length() - 2):
        p = [_dotb(x, x) for x in p]
        t_d = [t + _dotb(t, x) for t, x in zip(t_d, p)]
    if nb == 1:
        return t_d
    nn = [_dotb(t, x) for t, x in zip(t_d, e)]
    if nb == 2:
        return [t - _dotb(x, t) for t, x in zip(t_d, nn)]
    assert nb == 4, nb
    n2 = [_dotb(x, x) for x in nn]
    m1 = [t + _dotb(x, t) for t, x in zip(t_d, n2)]
    return [m - _dotb(x, m) for m, x in zip(m1, nn)]


def _seq_mixer_kernel(z_ref, zs_ref, hp_ref, hc_ref, s0_ref, convw_ref, poolw_ref, pscale_ref, alog_ref, dtb_ref,
                      nw_ref, mix_ref, s_ref, pstate_ref, cstate_ref, pext, cext, *, c, pos0):
    ci = pl.program_id(1)
    gh = HEAD_GROUP * c
    groups = range(N_HEADS // HEAD_GROUP)

    @pl.when(ci == 0)
    def _():
        pext[0:POOL_PAD, :] = hp_ref[0]
        cext[0:CONV_PAD, :] = hc_ref[0]
        s_ref[...] = s0_ref[...]

    pext[POOL_PAD:POOL_PAD + c, :] = z_ref[:, 0:POOL_WIDTH]
    cext[CONV_PAD:CONV_PAD + c, :] = z_ref[:, POOL_WIDTH:GATE_COL]

    pos = lax.broadcasted_iota(jnp.int32, (c, 1), 0) + ci * c + pos0
    for gi, w in enumerate(POOL_WINDOWS):
        cols = slice(gi * POOL_GROUP, (gi + 1) * POOL_GROUP)
        u = pext[POOL_PAD:POOL_PAD + c, cols]
        s = u
        for j in range(1, w):
            s = s + pext[POOL_PAD - j:POOL_PAD - j + c, cols]
        cnt = jnp.minimum(pos + 1, w).astype(F32)
        d = s / cnt - u
        y = jnp.dot(d.astype(BF16), poolw_ref[gi], preferred_element_type=F32)
        mix_ref[:, cols] = (y * pscale_ref[:, cols]).astype(mix_ref.dtype)

    zs = zs_ref[...]
    beta_all = jax.nn.sigmoid(zs)
    a = zs + dtb_ref[...]
    softplus = jnp.maximum(a, 0.0) + jnp.log1p(jnp.exp(-jnp.abs(a)))
    g_all = -jnp.exp(alog_ref[...]) * softplus
    rc = lax.broadcasted_iota(jnp.int32, (c, c), 0)
    qc = lax.broadcasted_iota(jnp.int32, (c, c), 1)
    gc_all = jnp.dot((rc >= qc).astype(F32), g_all, preferred_element_type=F32, precision=HIGHEST)

    def conv_block(col0):
        cols = slice(col0, col0 + HEAD_DIM)
        acc = convw_ref[0:1, cols] * cext[CONV_PAD - (CONV_W - 1):CONV_PAD - (CONV_W - 1) + c, cols]
        for j in range(1, CONV_W):
            lo = CONV_PAD - (CONV_W - 1) + j
            acc = acc + convw_ref[j:j + 1, cols] * cext[lo:lo + c, cols]
        return acc * jax.nn.sigmoid(acc)

    r2 = lax.broadcasted_iota(jnp.int32, (gh, gh), 0)
    q2 = lax.broadcasted_iota(jnp.int32, (gh, gh), 1)
    same_head = (r2 // c) == (q2 // c)
    incl = jnp.logical_and(same_head, r2 >= q2)
    strict = jnp.logical_and(same_head, r2 > q2)
    row_head = lax.broadcasted_iota(jnp.int32, (gh, LANES), 0) // c
    lane = lax.broadcasted_iota(jnp.int32, (gh, LANES), 1)
    gc_stack = jnp.concatenate([gc_all] * HEAD_GROUP, axis=0)
    beta_stack = jnp.concatenate([beta_all] * HEAD_GROUP, axis=0)
    ones_rows = jnp.ones((SUBLANES, LANES), F32)
    nt = (((1,), (1,)), ((), ()))
    tn = (((0,), (0,)), ((), ()))

    def stack(fn, g):
        return jnp.concatenate([fn(g * HEAD_GROUP + hh) for hh in range(HEAD_GROUP)], axis=0)

    def l2n(x):
        return x * lax.rsqrt(jnp.sum(x * x, -1, keepdims=True) + L2_EPS)

    q4 = [stack(lambda h: l2n(conv_block(h * HEAD_DIM)) * (HEAD_DIM ** -0.5), g) for g in groups]
    k4 = [stack(lambda h: l2n(conv_block(DN_WIDTH + h * HEAD_DIM)), g) for g in groups]
    v4 = [stack(lambda h: conv_block(2 * DN_WIDTH + h * HEAD_DIM), g) for g in groups]
    gsel = [jnp.where(lane == N_HEADS + g * HEAD_GROUP + row_head, gc_stack, 0.0) for g in groups]
    gcol = [jnp.sum(x, -1, keepdims=True) for x in gsel]
    grow = [lax.dot_general(ones_rows, x, nt, preferred_element_type=F32, precision=HIGHEST)[0:1, :] for x in gsel]
    beta = [jnp.sum(jnp.where(lane == g * HEAD_GROUP + row_head, beta_stack, 0.0), -1, keepdims=True) for g in groups]
    kb = [x.astype(BF16) for x in k4]
    qb = [x.astype(BF16) for x in q4]
    kk = [lax.dot_general(x, x, nt, preferred_element_type=F32) for x in kb]
    qk = [lax.dot_general(x, y, nt, preferred_element_type=F32) for x, y in zip(qb, kb)]
    dec = [jnp.exp(jnp.where(incl, gc - gr, -jnp.inf)) for gc, gr in zip(gcol, grow)]
    l_mat = [b * x * jnp.where(strict, d, 0.0) for b, x, d in zip(beta, kk, dec)]
    t_inv = _unit_lower_inverses(l_mat, gh, c // SOLVE_BLOCK)
    eg = [jnp.exp(x) for x in gcol]
    uw = [_dotb(t, jnp.concatenate([b * v, (b * e) * k], axis=1))
          for t, b, v, e, k in zip(t_inv, beta, v4, eg, k4)]
    rows = lambda x, hh: x[hh * c:(hh + 1) * c]
    s_old = [s_ref[0, h] for h in range(N_HEADS)]
    ws_qs = [[_dotb(jnp.concatenate([rows(uw[g], hh)[:, HEAD_DIM:], rows(q4[g], hh)], axis=0),
                    s_old[g * HEAD_GROUP + hh]) for hh in range(HEAD_GROUP)] for g in groups]
    u4 = [uw[g][:, :HEAD_DIM] - jnp.concatenate([x[:c] for x in ws_qs[g]], axis=0) for g in groups]
    qs4 = [jnp.concatenate([x[c:] for x in ws_qs[g]], axis=0) for g in groups]
    ub = [x.astype(BF16) for x in u4]
    o4 = [e * qs + jnp.dot((x * d).astype(BF16), u, preferred_element_type=F32)
          for e, qs, x, d, u in zip(eg, qs4, qk, dec, ub)]
    for g in groups:
        for hh in range(HEAD_GROUP):
            h = g * HEAD_GROUP + hh
            g_last = gc_all[c - 1:c, N_HEADS + h:N_HEADS + h + 1]
            tail = jnp.exp(g_last - rows(gcol[g], hh))
            s_ref[0, h] = jnp.exp(g_last) * s_old[h] + lax.dot_general(
                (rows(k4[g], hh) * tail).astype(BF16), rows(ub[g], hh), tn, preferred_element_type=F32)
            o = rows(o4[g], hh)
            o = o * lax.rsqrt(jnp.mean(o * o, -1, keepdims=True) + RMS_EPS) * nw_ref[...]
            gate = z_ref[:, GATE_COL + h * HEAD_DIM:GATE_COL + (h + 1) * HEAD_DIM]
            o = o * (gate * jax.nn.sigmoid(gate))
            mix_ref[:, POOL_WIDTH + h * HEAD_DIM:POOL_WIDTH + (h + 1) * HEAD_DIM] = o.astype(mix_ref.dtype)

    p_tail = pext[c:c + POOL_PAD, :]
    c_tail = cext[c:c + CONV_PAD, :]
    pext[0:POOL_PAD, :] = p_tail
    cext[0:CONV_PAD, :] = c_tail
    pstate_ref[0] = p_tail
    cstate_ref[0] = c_tail


def _seq_mixer(z, zs, row0, nseq, seqlen, c, pos0, hist_pool, hist_conv, s0, conv_w, pool_w, pool_scale, a_log_row,
               dt_bias_row, norm_w):
    nchunk = seqlen // c
    blk0 = row0 // c
    d_mix = POOL_WIDTH + DN_WIDTH
    row_map = lambda b, ci: (blk0 + b * nchunk + ci, 0)
    seq_map3 = lambda b, ci: (b, 0, 0)
    seq_map4 = lambda b, ci: (b, 0, 0, 0)
    const2 = lambda b, ci: (0, 0)
    kern = functools.partial(_seq_mixer_kernel, c=c, pos0=pos0)
    return pl.pallas_call(
        kern,
        grid=(nseq, nchunk),
        in_specs=[
            pl.BlockSpec((c, MAIN_WIDTH), row_map),
            pl.BlockSpec((c, LANES), row_map),
            pl.BlockSpec((1, POOL_PAD, POOL_WIDTH), seq_map3),
            pl.BlockSpec((1, CONV_PAD, QKV_WIDTH), seq_map3),
            pl.BlockSpec((1, N_HEADS, HEAD_DIM, HEAD_DIM), seq_map4),
            pl.BlockSpec((CONV_W, QKV_WIDTH), const2),
            pl.BlockSpec((len(POOL_WINDOWS), POOL_GROUP, POOL_GROUP), lambda b, ci: (0, 0, 0)),
            pl.BlockSpec((1, POOL_WIDTH), const2),
            pl.BlockSpec((1, LANES), const2),
            pl.BlockSpec((1, LANES), const2),
            pl.BlockSpec((1, HEAD_DIM), const2),
        ],
        out_specs=[
            pl.BlockSpec((c, d_mix), lambda b, ci: (b * nchunk + ci, 0)),
            pl.BlockSpec((1, N_HEADS, HEAD_DIM, HEAD_DIM), seq_map4),
            pl.BlockSpec((1, POOL_PAD, POOL_WIDTH), seq_map3),
            pl.BlockSpec((1, CONV_PAD, QKV_WIDTH), seq_map3),
        ],
        out_shape=[
            jax.ShapeDtypeStruct((nseq * seqlen, d_mix), BF16),
            jax.ShapeDtypeStruct((nseq, N_HEADS, HEAD_DIM, HEAD_DIM), F32),
            jax.ShapeDtypeStruct((nseq, POOL_PAD, POOL_WIDTH), F32),
            jax.ShapeDtypeStruct((nseq, CONV_PAD, QKV_WIDTH), F32),
        ],
        scratch_shapes=[pltpu.VMEM((POOL_PAD + c, POOL_WIDTH), F32), pltpu.VMEM((CONV_PAD + c, QKV_WIDTH), F32)],
        compiler_params=_cparams("parallel", "arbitrary"),
        name="seq_mixer",
    )(z, zs, hist_pool, hist_conv, s0, conv_w, pool_w, pool_scale, a_log_row, dt_bias_row, norm_w)


def _layer_norm_rows(x, g, b):
    xc = x - jnp.mean(x, -1, keepdims=True)
    var = jnp.mean(xc * xc, -1, keepdims=True)
    return xc * lax.rsqrt(var + LN_EPS) * g + b


def _pack_bf16_pairs(x):
    half = x.shape[1] // 2
    lo = lax.bitcast_convert_type(x[:, :half].astype(BF16).astype(F32), U32) >> 16
    hi = lax.bitcast_convert_type(x[:, half:].astype(BF16).astype(F32), U32) & jnp.uint32(HI_MASK)
    return lo | hi


def _outproj_ln_router_kernel(mix_ref, wout_ref, x_ref, g_ref, b_ref, wr_ref, br_ref, x1_ref, xp_ref, idx_ref,
                              gate_ref, *, alpha, n_experts):
    m = jnp.dot(mix_ref[...], wout_ref[...], preferred_element_type=F32)
    x1 = _layer_norm_rows(alpha * x_ref[...] + m, g_ref[...], b_ref[...])
    x1_ref[...] = x1
    words = _pack_bf16_pairs(x1)
    bm = x1.shape[0]
    sp = xp_ref.shape[0] // bm
    for s in range(sp):
        xp_ref[pl.ds(s, bm, stride=sp), :] = words[:, s * LANES:(s + 1) * LANES]
    logits = jnp.dot(x1, wr_ref[...], preferred_element_type=F32, precision=HIGHEST) + br_ref[...]
    lane = lax.broadcasted_iota(jnp.int32, logits.shape, 1).astype(F32)
    logits = jnp.where(lane < n_experts, logits, -jnp.inf)
    vals, idxs = [], []
    for _ in range(TOP_K):
        v = jnp.max(logits, -1, keepdims=True)
        i = jnp.min(jnp.where(logits == v, lane, float(LANES)), -1, keepdims=True)
        vals.append(v)
        idxs.append(i)
        logits = jnp.where(lane == i, -jnp.inf, logits)
    es = [jnp.exp(v - vals[0]) for v in vals]
    denom = es[0]
    for e in es[1:]:
        denom = denom + e
    idx_out = jnp.zeros(logits.shape, F32)
    gate_out = jnp.zeros(logits.shape, F32)
    for k in range(TOP_K):
        idx_out = jnp.where(lane == k, idxs[k], idx_out)
        gate_out = jnp.where(lane == k, es[k] / denom, gate_out)
    idx_ref[...] = idx_out.astype(jnp.int32)
    gate_ref[...] = gate_out


def _outproj_ln_router(mix, w_out, x, ln_g, ln_b, w_router_pad, b_router_pad, alpha, n_experts, bm):
    t, d = x.shape
    sp = d // (2 * LANES)
    row = lambda i: (i, 0)
    const = lambda i: (0, 0)
    kern = functools.partial(_outproj_ln_router_kernel, alpha=alpha, n_experts=n_experts)
    return pl.pallas_call(
        kern,
        grid=(t // bm,),
        in_specs=[
            pl.BlockSpec((bm, mix.shape[1]), row),
            pl.BlockSpec(w_out.shape, const),
            pl.BlockSpec((bm, d), row),
            pl.BlockSpec((1, d), const),
            pl.BlockSpec((1, d), const),
            pl.BlockSpec((d, LANES), const),
            pl.BlockSpec((1, LANES), const),
        ],
        out_specs=[pl.BlockSpec((bm, d), row), pl.BlockSpec((bm * sp, LANES), row),
                   pl.BlockSpec((bm, LANES), row), pl.BlockSpec((bm, LANES), row)],
        out_shape=[
            jax.ShapeDtypeStruct((t, d), F32),
            jax.ShapeDtypeStruct((t * sp, LANES), U32),
            jax.ShapeDtypeStruct((t, LANES), jnp.int32),
            jax.ShapeDtypeStruct((t, LANES), F32),
        ],
        compiler_params=_cparams("parallel"),
        name="outproj_ln_router",
    )(mix, w_out, x, ln_g, ln_b, w_router_pad, b_router_pad)


def _rank_kernel(idx_ref, rank_ref, count_ref, carry):
    @pl.when(pl.program_id(0) == 0)
    def _():
        carry[...] = jnp.zeros_like(carry)

    idx = idx_ref[...]
    bm = idx.shape[0]
    lane = lax.broadcasted_iota(jnp.int32, (bm, LANES), 1)
    onehots = [(idx[:, k:k + 1] == lane) for k in range(TOP_K)]
    cnt = onehots[0].astype(F32)
    for oh in onehots[1:]:
        cnt = cnt + oh.astype(F32)
    r = lax.broadcasted_iota(jnp.int32, (bm, bm), 0)
    q = lax.broadcasted_iota(jnp.int32, (bm, bm), 1)
    before = jnp.dot((r > q).astype(BF16), cnt.astype(BF16), preferred_element_type=F32) + carry[...]
    out = jnp.zeros((bm, LANES), jnp.int32)
    for k in range(TOP_K):
        rk = jnp.sum(jnp.where(onehots[k], before, 0.0), -1, keepdims=True).astype(jnp.int32)
        out = jnp.where(lane == k, rk, out)
    rank_ref[...] = out
    total = carry[...] + jnp.sum(cnt, 0, keepdims=True)
    carry[...] = total
    count_ref[...] = total.astype(jnp.int32)


def _rank(idx, bm):
    t = idx.shape[0]
    return pl.pallas_call(
        _rank_kernel,
        grid=(t // bm,),
        in_specs=[pl.BlockSpec((bm, LANES), lambda i: (i, 0))],
        out_specs=[pl.BlockSpec((bm, LANES), lambda i: (i, 0)), pl.BlockSpec((1, LANES), lambda i: (0, 0))],
        out_shape=[jax.ShapeDtypeStruct((t, LANES), jnp.int32), jax.ShapeDtypeStruct((1, LANES), jnp.int32)],
        scratch_shapes=[pltpu.VMEM((1, LANES), F32)],
        compiler_params=_cparams("arbitrary"),
        name="route_rank",
    )(idx)


def _token_rows(r, sp):
    return pl.ds(pl.multiple_of(r * sp, sp), sp)


def _dispatch_kernel(pos_ref, x_ref, xs_in_ref, xs_ref, sem, *, bm, sp):
    del xs_in_ref
    base = pl.program_id(0) * (bm * TOP_K)

    def copy(r, k):
        return pltpu.make_async_copy(x_ref.at[_token_rows(r, sp), :],
                                     xs_ref.at[_token_rows(pos_ref[base + r * TOP_K + k], sp), :], sem)

    def start(r, carry):
        for k in range(TOP_K):
            copy(r, k).start(priority=k % DMA_PRIORITIES)
        return carry

    def wait(r, carry):
        for k in range(TOP_K):
            copy(r, k).wait()
        return carry

    lax.fori_loop(0, bm, start, 0)
    lax.fori_loop(0, bm, wait, 0, unroll=2)


def _dispatch(xp, pos_flat, t, n_rows, bm):
    sp = xp.shape[0] // t
    xs0 = jnp.zeros((n_rows * sp, LANES), xp.dtype)
    return pl.pallas_call(
        functools.partial(_dispatch_kernel, bm=bm, sp=sp),
        grid_spec=pltpu.PrefetchScalarGridSpec(
            num_scalar_prefetch=1,
            grid=(t // bm,),
            in_specs=[pl.BlockSpec((bm * sp, LANES), lambda i, pos: (i, 0)), pl.BlockSpec(memory_space=pl.ANY)],
            out_specs=pl.BlockSpec(memory_space=pl.ANY),
            scratch_shapes=[pltpu.SemaphoreType.DMA(())],
        ),
        out_shape=jax.ShapeDtypeStruct((n_rows * sp, LANES), xp.dtype),
        input_output_aliases={2: 0},
        compiler_params=_cparams("arbitrary"),
        name="moe_dispatch",
    )(pos_flat, xp, xs0)


def _expert_mlp_kernel(te_ref, nr_ref, xs_ref, w1g_ref, w1l_ref, b1g_ref, b1l_ref, w2_ref, b2_ref, ys_ref, xb_ref,
                       acc_ref, wg_ref, wl_ref, w2b_ref, *, tm, subs):
    i = pl.program_id(0)
    j = pl.program_id(1)
    nj = pl.num_programs(1)
    n_rows = nr_ref[i]
    d = acc_ref.shape[1]
    half = d // 2
    sp = half // LANES
    so = d // LANES

    @pl.when(n_rows > 0)
    def _():
        wg_ref[...] = w1g_ref[0].astype(BF16)
        wl_ref[...] = w1l_ref[0].astype(BF16)
        w2b_ref[...] = w2_ref[0].astype(BF16)

    for sub in range(subs):
        r0 = sub * tm
        active = n_rows > r0

        @pl.when(jnp.logical_and(active, j == 0))
        def _():
            for s in range(sp):
                w = xs_ref[pl.ds(r0 * sp + s, tm, stride=sp), :]
                xb_ref[r0:r0 + tm, s * LANES:(s + 1) * LANES] = lax.bitcast_convert_type(w << 16, F32).astype(BF16)
                xb_ref[r0:r0 + tm, half + s * LANES:half + (s + 1) * LANES] = lax.bitcast_convert_type(
                    w & jnp.uint32(HI_MASK), F32).astype(BF16)
            acc_ref[r0:r0 + tm, :] = jnp.broadcast_to(b2_ref[0], (tm, acc_ref.shape[1]))

        @pl.when(active)
        def _():
            xb = xb_ref[r0:r0 + tm, :]
            ug = jnp.dot(xb, wg_ref[...], preferred_element_type=F32) + b1g_ref[0]
            ul = jnp.dot(xb, wl_ref[...], preferred_element_type=F32) + b1l_ref[0]
            x_glu = jnp.minimum(ug, SWIGLU_LIMIT)
            x_lin = jnp.clip(ul, -SWIGLU_LIMIT, SWIGLU_LIMIT)
            act = x_glu * jax.nn.sigmoid(SWIGLU_ALPHA * x_glu) * (x_lin + 1.0)
            acc_ref[r0:r0 + tm, :] += jnp.dot(act.astype(BF16), w2b_ref[...], preferred_element_type=F32)

        @pl.when(jnp.logical_and(active, j == nj - 1))
        def _():
            for s in range(so):
                ys_ref[pl.ds(r0 * so + s, tm, stride=so), :] = acc_ref[r0:r0 + tm, s * LANES:(s + 1) * LANES]

        @pl.when(jnp.logical_and(jnp.logical_not(active), j == nj - 1))
        def _():
            ys_ref[r0 * so:(r0 + tm) * so, :] = jnp.zeros((tm * so, LANES), ys_ref.dtype)


def _expert_mlp(xs, tile_expert, tile_rows, w1, b1, w2, b2, tm, subs, th):
    n_we, d, two_h = w1.shape
    sp = d // (2 * LANES)
    so = d // LANES
    n_rows = xs.shape[0] // sp
    hdim = two_h // 2
    nj = hdim // th
    tb = tm * subs
    n_tiles = n_rows // tb
    b1r = b1.reshape(n_we, 1, two_h)
    b2r = b2.reshape(n_we, 1, d)

    def jj(i, j, nr):
        return jnp.where(nr[i] > 0, j, nj - 1)

    return pl.pallas_call(
        functools.partial(_expert_mlp_kernel, tm=tm, subs=subs),
        grid_spec=pltpu.PrefetchScalarGridSpec(
            num_scalar_prefetch=2,
            grid=(n_tiles, nj),
            in_specs=[
                pl.BlockSpec((tb * sp, LANES), lambda i, j, te, nr: (i, 0)),
                pl.BlockSpec((1, d, th), lambda i, j, te, nr: (te[i], 0, jj(i, j, nr))),
                pl.BlockSpec((1, d, th), lambda i, j, te, nr: (te[i], 0, nj + jj(i, j, nr))),
                pl.BlockSpec((1, 1, th), lambda i, j, te, nr: (te[i], 0, jj(i, j, nr))),
                pl.BlockSpec((1, 1, th), lambda i, j, te, nr: (te[i], 0, nj + jj(i, j, nr))),
                pl.BlockSpec((1, th, d), lambda i, j, te, nr: (te[i], jj(i, j, nr), 0)),
                pl.BlockSpec((1, 1, d), lambda i, j, te, nr: (te[i], 0, 0)),
            ],
            out_specs=pl.BlockSpec((tb * so, LANES), lambda i, j, te, nr: (i, 0)),
            scratch_shapes=[pltpu.VMEM((tb, d), BF16), pltpu.VMEM((tb, d), F32), pltpu.VMEM((d, th), BF16),
                            pltpu.VMEM((d, th), BF16), pltpu.VMEM((th, d), BF16)],
        ),
        out_shape=jax.ShapeDtypeStruct((n_rows * so, LANES), F32),
        compiler_params=_cparams("arbitrary", "arbitrary"),
        name="expert_mlp",
    )(tile_expert, tile_rows, xs, w1, w1, b1r, b1r, w2, b2r)


def _combine_ln_kernel(pos_ref, ys_ref, gate_ref, x_ref, g_ref, b_ref, o_ref, ob_ref, buf, acc_ref, sem, *, bm,
                       alpha):
    base = pl.program_id(0) * (bm * TOP_K)
    so = acc_ref.shape[1] // LANES

    hb = bm // COMBINE_PARTS

    def copy(r, k, part):
        return pltpu.make_async_copy(ys_ref.at[_token_rows(pos_ref[base + r * TOP_K + k], so), :],
                                     buf.at[k, _token_rows(r, so), :], sem.at[part])

    def start(part):
        def body(r, carry):
            for k in range(TOP_K):
                copy(r, k, part).start(priority=k % DMA_PRIORITIES)
            return carry
        return body

    def wait(part):
        def body(r, carry):
            for k in range(TOP_K):
                copy(r, k, part).wait()
            return carry
        return body

    for part in range(COMBINE_PARTS):
        lax.fori_loop(part * hb, (part + 1) * hb, start(part), 0)
    for part in range(COMBINE_PARTS):
        r0 = part * hb
        lax.fori_loop(r0, r0 + hb, wait(part), 0, unroll=2)
        gates = gate_ref[r0:r0 + hb, :]
        for s in range(so):
            acc = gates[:, 0:1] * buf[0, pl.ds(r0 * so + s, hb, stride=so), :]
            for k in range(1, TOP_K):
                acc = acc + gates[:, k:k + 1] * buf[k, pl.ds(r0 * so + s, hb, stride=so), :]
            acc_ref[r0:r0 + hb, s * LANES:(s + 1) * LANES] = acc
        y = _layer_norm_rows(alpha * x_ref[r0:r0 + hb, :] + acc_ref[r0:r0 + hb, :], g_ref[...], b_ref[...])
        o_ref[r0:r0 + hb, :] = y
        ob_ref[r0:r0 + hb, :] = y.astype(BF16)


def _combine_ln(ys, pos_flat, gates, x, ln_g, ln_b, alpha, bm):
    t, d = x.shape
    row = lambda i, pos: (i, 0)
    const = lambda i, pos: (0, 0)
    return pl.pallas_call(
        functools.partial(_combine_ln_kernel, bm=bm, alpha=alpha),
        grid_spec=pltpu.PrefetchScalarGridSpec(
            num_scalar_prefetch=1,
            grid=(t // bm,),
            in_specs=[
                pl.BlockSpec(memory_space=pl.ANY),
                pl.BlockSpec((bm, LANES), row),
                pl.BlockSpec((bm, d), row),
                pl.BlockSpec((1, d), const),
                pl.BlockSpec((1, d), const),
            ],
            out_specs=[pl.BlockSpec((bm, d), row), pl.BlockSpec((bm, d), row)],
            scratch_shapes=[pltpu.VMEM((TOP_K, bm * (d // LANES), LANES), F32), pltpu.VMEM((bm, d), F32),
                            pltpu.SemaphoreType.DMA((COMBINE_PARTS,))],
        ),
        out_shape=[jax.ShapeDtypeStruct((t, d), F32), jax.ShapeDtypeStruct((t, d), BF16)],
        compiler_params=_cparams("arbitrary"),
        name="moe_combine_ln",
    )(pos_flat, ys, gates, x, ln_g, ln_b)


def _moe_ln(x1, xp, idx, gates, layer, n_exp, w1, b1, w2, b2, ln_g, ln_b, alpha, tm, subs, th, bm_rank, bm_move):
    t, d = x1.shape
    tb = tm * subs
    rank, counts = _rank(idx, bm_rank)
    counts = counts[0, :n_exp]
    padded = ((counts + tb - 1) // tb) * tb
    ends = jnp.cumsum(padded)
    offs = ends - padded
    top = idx[:, :TOP_K]
    pos = (offs[top] + rank[:, :TOP_K]).reshape(-1).astype(jnp.int32)
    n_tiles = -(-(t * TOP_K) // tb) + n_exp
    starts = jnp.arange(n_tiles, dtype=jnp.int32) * tb
    tile_expert = jnp.minimum(jnp.sum((ends[None, :] <= starts[:, None]).astype(jnp.int32), axis=1), n_exp - 1)
    tile_rows = jnp.clip(counts[tile_expert] - (starts - offs[tile_expert]), 0, tb)
    tile_rows = jnp.where(starts < ends[-1], tile_rows, 0).astype(jnp.int32)
    last_valid = jnp.maximum(ends[-1] // tb - 1, 0)
    tile_expert = jnp.where(starts < ends[-1], tile_expert, tile_expert[last_valid])
    tile_expert = (tile_expert + layer * n_exp).astype(jnp.int32)
    xs = _dispatch(xp, pos, t, n_tiles * tb, bm_move)
    ys = _expert_mlp(xs, tile_expert, tile_rows, w1, b1, w2, b2, tm, subs, th)
    return _combine_ln(ys, pos, gates, x1, ln_g, ln_b, alpha, bm_move)


def _pad_lanes(v, offset=0, fill=0.0):
    out = jnp.full((1, LANES), fill, F32)
    return out.at[0, offset:offset + v.shape[0]].set(v.astype(F32))


def _tiles(t):
    def pick(cands):
        for c in cands:
            if t % c == 0:
                return c
        raise ValueError(t)
    big = t >= 4096
    return dict(bm_proj=pick((1024, 512, 256, 64)), bm_ln=pick((256, 64)), bm_rank=pick((512, 256, 64)),
                bm_move=pick((512, 256, 64)), tm=512 if big else 64, subs=2)


def kernel(x_prompt, x_sample, state_pool, state_conv, state_delta, w_in, conv_w, pool_w, pool_scale, a_log, dt_bias,
           dn_norm_w, w_out, ln1_g, ln1_b, w_router, b_router, w1, b1, w2, b2, ln2_g, ln2_b):
    bp, lp, d = x_prompt.shape
    bs, ls, _ = x_sample.shape
    depth = w_in.shape[0]
    n_exp = w_router.shape[-1]
    alpha = float((2 * depth) ** 0.25)
    tp, ts = bp * lp, bs * ls
    t = tp + ts
    tl = _tiles(t)
    c_p = CHUNK if lp % CHUNK == 0 else lp
    c_s = CHUNK if ls % CHUNK == 0 else ls
    th = 256
    assert lp >= POOL_HIST and ls >= POOL_HIST

    x = jnp.concatenate([x_prompt.reshape(tp, d), x_sample.reshape(ts, d)], axis=0)
    xb = x.astype(BF16)
    zero_pool = jnp.zeros((bp, POOL_PAD, POOL_WIDTH), F32)
    zero_conv = jnp.zeros((bp, CONV_PAD, QKV_WIDTH), F32)
    zero_state = jnp.zeros((bp, N_HEADS, HEAD_DIM, HEAD_DIM), F32)
    w1s = w1.reshape((depth * n_exp,) + w1.shape[2:])
    b1s = b1.reshape((depth * n_exp,) + b1.shape[2:])
    w2s = w2.reshape((depth * n_exp,) + w2.shape[2:])
    b2s = b2.reshape((depth * n_exp,) + b2.shape[2:])
    outs = {k: [] for k in ("pp", "pc", "pd", "sp", "sc", "sd")}
    for l in range(depth):
        w_main = w_in[l][:, :MAIN_WIDTH].astype(BF16)
        w_small = jnp.pad(w_in[l][:, MAIN_WIDTH:], ((0, 0), (0, LANES - 2 * N_HEADS))).astype(BF16)
        z = _matmul(xb, w_main, tl["bm_proj"], 512)
        zs = _matmul(xb, w_small, tl["bm_proj"], LANES)
        a_log_row = _pad_lanes(a_log[l], N_HEADS)
        dt_row = _pad_lanes(dt_bias[l], N_HEADS)
        seq_w = (conv_w[l], pool_w[l].astype(BF16), pool_scale[l].reshape(1, -1), a_log_row, dt_row,
                 dn_norm_w[l].reshape(1, -1))
        hp = jnp.pad(state_pool[l], ((0, 0), (POOL_PAD - POOL_HIST, 0), (0, 0)))
        hc = jnp.pad(state_conv[l], ((0, 0), (CONV_PAD - (CONV_W - 1), 0), (0, 0)))
        mix_p, sd_p, ps_p, cs_p = _seq_mixer(z, zs, 0, bp, lp, c_p, 0, zero_pool, zero_conv, zero_state, *seq_w)
        mix_s, sd_s, ps_s, cs_s = _seq_mixer(z, zs, tp, bs, ls, c_s, PAST_LEN, hp, hc, state_delta[l], *seq_w)
        mix = jnp.concatenate([mix_p, mix_s], axis=0)
        outs["pp"].append(ps_p[:, POOL_PAD - POOL_HIST:])
        outs["pc"].append(cs_p[:, CONV_PAD - (CONV_W - 1):])
        outs["pd"].append(sd_p)
        outs["sp"].append(ps_s[:, POOL_PAD - POOL_HIST:])
        outs["sc"].append(cs_s[:, CONV_PAD - (CONV_W - 1):])
        outs["sd"].append(sd_s)
        wr = jnp.pad(w_router[l], ((0, 0), (0, LANES - n_exp)))
        br = _pad_lanes(b_router[l])
        x1, xp, idx, gates = _outproj_ln_router(mix, w_out[l].astype(BF16), x, ln1_g[l].reshape(1, -1),
                                                ln1_b[l].reshape(1, -1), wr, br, alpha, n_exp, tl["bm_ln"])
        x, xb = _moe_ln(x1, xp, idx, gates, l, n_exp, w1s, b1s, w2s, b2s, ln2_g[l].reshape(1, -1),
                        ln2_b[l].reshape(1, -1), alpha, tl["tm"], tl["subs"], th, tl["bm_rank"], tl["bm_move"])
    y_prompt = x[:tp].reshape(bp, lp, d)
    y_sample = x[tp:].reshape(bs, ls, d)
    st = lambda k: jnp.stack(outs[k])
    return (y_prompt, y_sample, st("pp"), st("pc"), st("pd"), st("sp"), st("sc"), st("sd"))
```

```python
import functools

import jax
import jax.numpy as jnp
from jax import lax
from jax.experimental import pallas as pl
from jax.experimental.pallas import tpu as pltpu

F32 = jnp.float32
BF16 = jnp.bfloat16
U32 = jnp.uint32
HIGHEST = lax.Precision.HIGHEST

POOL_WINDOWS = (2, 4, 8, 16)
POOL_GROUP = 128
POOL_WIDTH = POOL_GROUP * len(POOL_WINDOWS)
POOL_HIST = max(POOL_WINDOWS) - 1
HEAD_DIM = 128
N_HEADS = 12
DN_WIDTH = N_HEADS * HEAD_DIM
QKV_WIDTH = 3 * DN_WIDTH
CONV_W = 4
MAIN_WIDTH = POOL_WIDTH + QKV_WIDTH + DN_WIDTH
GATE_COL = POOL_WIDTH + QKV_WIDTH
CHUNK = 64
PAST_LEN = 4096
TOP_K = 4
SWIGLU_LIMIT = 7.0
SWIGLU_ALPHA = 1.702
LN_EPS = 1e-5
RMS_EPS = 1e-6
L2_EPS = 1e-6

LANES = 128
SUBLANES = 8
SOLVE_BLOCK = 16
HEAD_GROUP = 4
POOL_PAD = 16
CONV_PAD = 8
VMEM_LIMIT = 56 * 1024 * 1024
HI_MASK = 0xFFFF0000
DMA_PRIORITIES = 2
COMBINE_PARTS = 2


def _cparams(*sem):
    return pltpu.CompilerParams(dimension_semantics=sem, vmem_limit_bytes=VMEM_LIMIT)


def _dotb(a, b):
    return jnp.dot(a.astype(BF16), b.astype(BF16), preferred_element_type=F32)


def _matmul_kernel(x_ref, w_ref, o_ref):
    o_ref[...] = jnp.dot(x_ref[...], w_ref[...], preferred_element_type=F32).astype(o_ref.dtype)


def _matmul(x, w, bm, bn, out_dtype=F32):
    m, k = x.shape
    n = w.shape[1]
    return pl.pallas_call(
        _matmul_kernel,
        grid=(m // bm, n // bn),
        in_specs=[pl.BlockSpec((bm, k), lambda i, j: (i, 0)), pl.BlockSpec((k, bn), lambda i, j: (0, j))],
        out_specs=pl.BlockSpec((bm, bn), lambda i, j: (i, j)),
        out_shape=jax.ShapeDtypeStruct((m, n), out_dtype),
        compiler_params=_cparams("parallel", "arbitrary"),
        name="in_proj",
    )(x, w)


def _unit_lower_inverses(ls, n, nb):
    r = lax.broadcasted_iota(jnp.int32, (n, n), 0)
    q = lax.broadcasted_iota(jnp.int32, (n, n), 1)
    eye = (r == q).astype(F32)
    same = (r // SOLVE_BLOCK) == (q // SOLVE_BLOCK)
    p = [jnp.where(same, l, 0.0) for l in ls]
    e = [jnp.where(same, 0.0, l) for l in ls]
    t_d = [eye - x for x in p]
    for _ in range(SOLVE_BLOCK.bit_length() - 2):
        p = [_dotb(x, x) for x in p]
        t_d = [t + _dotb(t, x) for t, x in zip(t_d, p)]
    if nb == 1:
        return t_d
    nn = [_dotb(t, x) for t, x in zip(t_d, e)]
    if nb == 2:
        return [t - _dotb(x, t) for t, x in zip(t_d, nn)]
    assert nb == 4, nb
    n2 = [_dotb(x, x) for x in nn]
    m1 = [t + _dotb(x, t) for t, x in zip(t_d, n2)]
    return [m - _dotb(x, m) for m, x in zip(m1, nn)]


def _seq_mixer_kernel(z_ref, zs_ref, hp_ref, hc_ref, s0_ref, convw_ref, poolw_ref, pscale_ref, alog_ref, dtb_ref,
                      nw_ref, mix_ref, s_ref, pstate_ref, cstate_ref, pext, cext, *, c, pos0):
    ci = pl.program_id(1)
    gh = HEAD_GROUP * c
    groups = range(N_HEADS // HEAD_GROUP)

    @pl.when(ci == 0)
    def _():
        pext[0:POOL_PAD, :] = hp_ref[0]
        cext[0:CONV_PAD, :] = hc_ref[0]
        s_ref[...] = s0_ref[...]

    pext[POOL_PAD:POOL_PAD + c, :] = z_ref[:, 0:POOL_WIDTH]
    cext[CONV_PAD:CONV_PAD + c, :] = z_ref[:, POOL_WIDTH:GATE_COL]

    pos = lax.broadcasted_iota(jnp.int32, (c, 1), 0) + ci * c + pos0
    for gi, w in enumerate(POOL_WINDOWS):
        cols = slice(gi * POOL_GROUP, (gi + 1) * POOL_GROUP)
        u = pext[POOL_PAD:POOL_PAD + c, cols]
        s = u
        for j in range(1, w):
            s = s + pext[POOL_PAD - j:POOL_PAD - j + c, cols]
        cnt = jnp.minimum(pos + 1, w).astype(F32)
        d = s / cnt - u
        y = jnp.dot(d.astype(BF16), poolw_ref[gi], preferred_element_type=F32)
        mix_ref[:, cols] = (y * pscale_ref[:, cols]).astype(mix_ref.dtype)

    zs = zs_ref[...]
    beta_all = jax.nn.sigmoid(zs)
    a = zs + dtb_ref[...]
    softplus = jnp.maximum(a, 0.0) + jnp.log1p(jnp.exp(-jnp.abs(a)))
    g_all = -jnp.exp(alog_ref[...]) * softplus
    rc = lax.broadcasted_iota(jnp.int32, (c, c), 0)
    qc = lax.broadcasted_iota(jnp.int32, (c, c), 1)
    gc_all = jnp.dot((rc >= qc).astype(F32), g_all, preferred_element_type=F32, precision=HIGHEST)

    def conv_block(col0):
        cols = slice(col0, col0 + HEAD_DIM)
        acc = convw_ref[0:1, cols] * cext[CONV_PAD - (CONV_W - 1):CONV_PAD - (CONV_W - 1) + c, cols]
        for j in range(1, CONV_W):
            lo = CONV_PAD - (CONV_W - 1) + j
            acc = acc + convw_ref[j:j + 1, cols] * cext[lo:lo + c, cols]
        return acc * jax.nn.sigmoid(acc)

    r2 = lax.broadcasted_iota(jnp.int32, (gh, gh), 0)
    q2 = lax.broadcasted_iota(jnp.int32, (gh, gh), 1)
    same_head = (r2 // c) == (q2 // c)
    incl = jnp.logical_and(same_head, r2 >= q2)
    strict = jnp.logical_and(same_head, r2 > q2)
    row_head = lax.broadcasted_iota(jnp.int32, (gh, LANES), 0) // c
    lane = lax.broadcasted_iota(jnp.int32, (gh, LANES), 1)
    gc_stack = jnp.concatenate([gc_all] * HEAD_GROUP, axis=0)
    beta_stack = jnp.concatenate([beta_all] * HEAD_GROUP, axis=0)
    ones_rows = jnp.ones((SUBLANES, LANES), F32)
    nt = (((1,), (1,)), ((), ()))
    tn = (((0,), (0,)), ((), ()))

    def stack(fn, g):
        return jnp.concatenate([fn(g * HEAD_GROUP + hh) for hh in range(HEAD_GROUP)], axis=0)

    def l2n(x):
        return x * lax.rsqrt(jnp.sum(x * x, -1, keepdims=True) + L2_EPS)

    q4 = [stack(lambda h: l2n(conv_block(h * HEAD_DIM)) * (HEAD_DIM ** -0.5), g) for g in groups]
    k4 = [stack(lambda h: l2n(conv_block(DN_WIDTH + h * HEAD_DIM)), g) for g in groups]
    v4 = [stack(lambda h: conv_block(2 * DN_WIDTH + h * HEAD_DIM), g) for g in groups]
    gsel = [jnp.where(lane == N_HEADS + g * HEAD_GROUP + row_head, gc_stack, 0.0) for g in groups]
    gcol = [jnp.sum(x, -1, keepdims=True) for x in gsel]
    grow = [lax.dot_general(ones_rows, x, nt, preferred_element_type=F32, precision=HIGHEST)[0:1, :] for x in gsel]
    beta = [jnp.sum(jnp.where(lane == g * HEAD_GROUP + row_head, beta_stack, 0.0), -1, keepdims=True) for g in groups]
    kb = [x.astype(BF16) for x in k4]
    qb = [x.astype(BF16) for x in q4]
    kk = [lax.dot_general(x, x, nt, preferred_element_type=F32) for x in kb]
    qk = [lax.dot_general(x, y, nt, preferred_element_type=F32) for x, y in zip(qb, kb)]
    dec = [jnp.exp(jnp.where(incl, gc - gr, -jnp.inf)) for gc, gr in zip(gcol, grow)]
    l_mat = [b * x * jnp.where(strict, d, 0.0) for b, x, d in zip(beta, kk, dec)]
    t_inv = _unit_lower_inverses(l_mat, gh, c // SOLVE_BLOCK)
    eg = [jnp.exp(x) for x in gcol]
    uw = [_dotb(t, jnp.concatenate([b * v, (b * e) * k], axis=1))
          for t, b, v, e, k in zip(t_inv, beta, v4, eg, k4)]
    rows = lambda x, hh: x[hh * c:(hh + 1) * c]
    s_old = [s_ref[0, h] for h in range(N_HEADS)]
    ws_qs = [[_dotb(jnp.concatenate([rows(uw[g], hh)[:, HEAD_DIM:], rows(q4[g], hh)], axis=0),
                    s_old[g * HEAD_GROUP + hh]) for hh in range(HEAD_GROUP)] for g in groups]
    u4 = [uw[g][:, :HEAD_DIM] - jnp.concatenate([x[:c] for x in ws_qs[g]], axis=0) for g in groups]
    qs4 = [jnp.concatenate([x[c:] for x in ws_qs[g]], axis=0) for g in groups]
    ub = [x.astype(BF16) for x in u4]
    o4 = [e * qs + jnp.dot((x * d).astype(BF16), u, preferred_element_type=F32)
          for e, qs, x, d, u in zip(eg, qs4, qk, dec, ub)]
    for g in groups:
        for hh in range(HEAD_GROUP):
            h = g * HEAD_GROUP + hh
            g_last = gc_all[c - 1:c, N_HEADS + h:N_HEADS + h + 1]
            tail = jnp.exp(g_last - rows(gcol[g], hh))
            s_ref[0, h] = jnp.exp(g_last) * s_old[h] + lax.dot_general(
                (rows(k4[g], hh) * tail).astype(BF16), rows(ub[g], hh), tn, preferred_element_type=F32)
            o = rows(o4[g], hh)
            o = o * lax.rsqrt(jnp.mean(o * o, -1, keepdims=True) + RMS_EPS) * nw_ref[...]
            gate = z_ref[:, GATE_COL + h * HEAD_DIM:GATE_COL + (h + 1) * HEAD_DIM]
            o = o * (gate * jax.nn.sigmoid(gate))
            mix_ref[:, POOL_WIDTH + h * HEAD_DIM:POOL_WIDTH + (h + 1) * HEAD_DIM] = o.astype(mix_ref.dtype)

    p_tail = pext[c:c + POOL_PAD, :]
    c_tail = cext[c:c + CONV_PAD, :]
    pext[0:POOL_PAD, :] = p_tail
    cext[0:CONV_PAD, :] = c_tail
    pstate_ref[0] = p_tail
    cstate_ref[0] = c_tail


def _seq_mixer(z, zs, row0, nseq, seqlen, c, pos0, hist_pool, hist_conv, s0, conv_w, pool_w, pool_scale, a_log_row,
               dt_bias_row, norm_w):
    nchunk = seqlen // c
    blk0 = row0 // c
    d_mix = POOL_WIDTH + DN_WIDTH
    row_map = lambda b, ci: (blk0 + b * nchunk + ci, 0)
    seq_map3 = lambda b, ci: (b, 0, 0)
    seq_map4 = lambda b, ci: (b, 0, 0, 0)
    const2 = lambda b, ci: (0, 0)
    kern = functools.partial(_seq_mixer_kernel, c=c, pos0=pos0)
    return pl.pallas_call(
        kern,
        grid=(nseq, nchunk),
        in_specs=[
            pl.BlockSpec((c, MAIN_WIDTH), row_map),
            pl.BlockSpec((c, LANES), row_map),
            pl.BlockSpec((1, POOL_PAD, POOL_WIDTH), seq_map3),
            pl.BlockSpec((1, CONV_PAD, QKV_WIDTH), seq_map3),
            pl.BlockSpec((1, N_HEADS, HEAD_DIM, HEAD_DIM), seq_map4),
            pl.BlockSpec((CONV_W, QKV_WIDTH), const2),
            pl.BlockSpec((len(POOL_WINDOWS), POOL_GROUP, POOL_GROUP), lambda b, ci: (0, 0, 0)),
            pl.BlockSpec((1, POOL_WIDTH), const2),
            pl.BlockSpec((1, LANES), const2),
            pl.BlockSpec((1, LANES), const2),
            pl.BlockSpec((1, HEAD_DIM), const2),
        ],
        out_specs=[
            pl.BlockSpec((c, d_mix), lambda b, ci: (b * nchunk + ci, 0)),
            pl.BlockSpec((1, N_HEADS, HEAD_DIM, HEAD_DIM), seq_map4),
            pl.BlockSpec((1, POOL_PAD, POOL_WIDTH), seq_map3),
            pl.BlockSpec((1, CONV_PAD, QKV_WIDTH), seq_map3),
        ],
        out_shape=[
            jax.ShapeDtypeStruct((nseq * seqlen, d_mix), BF16),
            jax.ShapeDtypeStruct((nseq, N_HEADS, HEAD_DIM, HEAD_DIM), F32),
            jax.ShapeDtypeStruct((nseq, POOL_PAD, POOL_WIDTH), F32),
            jax.ShapeDtypeStruct((nseq, CONV_PAD, QKV_WIDTH), F32),
        ],
        scratch_shapes=[pltpu.VMEM((POOL_PAD + c, POOL_WIDTH), F32), pltpu.VMEM((CONV_PAD + c, QKV_WIDTH), F32)],
        compiler_params=_cparams("parallel", "arbitrary"),
        name="seq_mixer",
    )(z, zs, hist_pool, hist_conv, s0, conv_w, pool_w, pool_scale, a_log_row, dt_bias_row, norm_w)


def _layer_norm_rows(x, g, b):
    xc = x - jnp.mean(x, -1, keepdims=True)
    var = jnp.mean(xc * xc, -1, keepdims=True)
    return xc * lax.rsqrt(var + LN_EPS) * g + b


def _pack_bf16_pairs(x):
    half = x.shape[1] // 2
    lo = lax.bitcast_convert_type(x[:, :half].astype(BF16).astype(F32), U32) >> 16
    hi = lax.bitcast_convert_type(x[:, half:].astype(BF16).astype(F32), U32) & jnp.uint32(HI_MASK)
    return lo | hi


def _outproj_ln_router_kernel(mix_ref, wout_ref, x_ref, g_ref, b_ref, wr_ref, br_ref, x1_ref, xp_ref, idx_ref,
                              gate_ref, *, alpha, n_experts):
    m = jnp.dot(mix_ref[...], wout_ref[...], preferred_element_type=F32)
    x1 = _layer_norm_rows(alpha * x_ref[...] + m, g_ref[...], b_ref[...])
    x1_ref[...] = x1
    words = _pack_bf16_pairs(x1)
    bm = x1.shape[0]
    sp = xp_ref.shape[0] // bm
    for s in range(sp):
        xp_ref[pl.ds(s, bm, stride=sp), :] = words[:, s * LANES:(s + 1) * LANES]
    logits = jnp.dot(x1, wr_ref[...], preferred_element_type=F32, precision=HIGHEST) + br_ref[...]
    lane = lax.broadcasted_iota(jnp.int32, logits.shape, 1).astype(F32)
    logits = jnp.where(lane < n_experts, logits, -jnp.inf)
    vals, idxs = [], []
    for _ in range(TOP_K):
        v = jnp.max(logits, -1, keepdims=True)
        i = jnp.min(jnp.where(logits == v, lane, float(LANES)), -1, keepdims=True)
        vals.append(v)
        idxs.append(i)
        logits = jnp.where(lane == i, -jnp.inf, logits)
    es = [jnp.exp(v - vals[0]) for v in vals]
    denom = es[0]
    for e in es[1:]:
        denom = denom + e
    idx_out = jnp.zeros(logits.shape, F32)
    gate_out = jnp.zeros(logits.shape, F32)
    for k in range(TOP_K):
        idx_out = jnp.where(lane == k, idxs[k], idx_out)
        gate_out = jnp.where(lane == k, es[k] / denom, gate_out)
    idx_ref[...] = idx_out.astype(jnp.int32)
    gate_ref[...] = gate_out


def _outproj_ln_router(mix, w_out, x, ln_g, ln_b, w_router_pad, b_router_pad, alpha, n_experts, bm):
    t, d = x.shape
    sp = d // (2 * LANES)
    row = lambda i: (i, 0)
    const = lambda i: (0, 0)
    kern = functools.partial(_outproj_ln_router_kernel, alpha=alpha, n_experts=n_experts)
    return pl.pallas_call(
        kern,
        grid=(t // bm,),
        in_specs=[
            pl.BlockSpec((bm, mix.shape[1]), row),
            pl.BlockSpec(w_out.shape, const),
            pl.BlockSpec((bm, d), row),
            pl.BlockSpec((1, d), const),
            pl.BlockSpec((1, d), const),
            pl.BlockSpec((d, LANES), const),
            pl.BlockSpec((1, LANES), const),
        ],
        out_specs=[pl.BlockSpec((bm, d), row), pl.BlockSpec((bm * sp, LANES), row),
                   pl.BlockSpec((bm, LANES), row), pl.BlockSpec((bm, LANES), row)],
        out_shape=[
            jax.ShapeDtypeStruct((t, d), F32),
            jax.ShapeDtypeStruct((t * sp, LANES), U32),
            jax.ShapeDtypeStruct((t, LANES), jnp.int32),
            jax.ShapeDtypeStruct((t, LANES), F32),
        ],
        compiler_params=_cparams("parallel"),
        name="outproj_ln_router",
    )(mix, w_out, x, ln_g, ln_b, w_router_pad, b_router_pad)


def _rank_kernel(idx_ref, rank_ref, count_ref, carry):
    @pl.when(pl.program_id(0) == 0)
    def _():
        carry[...] = jnp.zeros_like(carry)

    idx = idx_ref[...]
    bm = idx.shape[0]
    lane = lax.broadcasted_iota(jnp.int32, (bm, LANES), 1)
    onehots = [(idx[:, k:k + 1] == lane) for k in range(TOP_K)]
    cnt = onehots[0].astype(F32)
    for oh in onehots[1:]:
        cnt = cnt + oh.astype(F32)
    r = lax.broadcasted_iota(jnp.int32, (bm, bm), 0)
    q = lax.broadcasted_iota(jnp.int32, (bm, bm), 1)
    before = jnp.dot((r > q).astype(BF16), cnt.astype(BF16), preferred_element_type=F32) + carry[...]
    out = jnp.zeros((bm, LANES), jnp.int32)
    for k in range(TOP_K):
        rk = jnp.sum(jnp.where(onehots[k], before, 0.0), -1, keepdims=True).astype(jnp.int32)
        out = jnp.where(lane == k, rk, out)
    rank_ref[...] = out
    total = carry[...] + jnp.sum(cnt, 0, keepdims=True)
    carry[...] = total
    count_ref[...] = total.astype(jnp.int32)


def _rank(idx, bm):
    t = idx.shape[0]
    return pl.pallas_call(
        _rank_kernel,
        grid=(t // bm,),
        in_specs=[pl.BlockSpec((bm, LANES), lambda i: (i, 0))],
        out_specs=[pl.BlockSpec((bm, LANES), lambda i: (i, 0)), pl.BlockSpec((1, LANES), lambda i: (0, 0))],
        out_shape=[jax.ShapeDtypeStruct((t, LANES), jnp.int32), jax.ShapeDtypeStruct((1, LANES), jnp.int32)],
        scratch_shapes=[pltpu.VMEM((1, LANES), F32)],
        compiler_params=_cparams("arbitrary"),
        name="route_rank",
    )(idx)


def _token_rows(r, sp):
    return pl.ds(pl.multiple_of(r * sp, sp), sp)


def _dispatch_kernel(pos_ref, x_ref, xs_in_ref, xs_ref, sem, *, bm, sp):
    del xs_in_ref
    base = pl.program_id(0) * (bm * TOP_K)

    def copy(r, k):
        return pltpu.make_async_copy(x_ref.at[_token_rows(r, sp), :],
                                     xs_ref.at[_token_rows(pos_ref[base + r * TOP_K + k], sp), :], sem)

    def start(r, carry):
        for k in range(TOP_K):
            copy(r, k).start(priority=k % DMA_PRIORITIES)
        return carry

    def wait(r, carry):
        for k in range(TOP_K):
            copy(r, k).wait()
        return carry

    lax.fori_loop(0, bm, start, 0)
    lax.fori_loop(0, bm, wait, 0, unroll=2)


def _dispatch(xp, pos_flat, t, n_rows, bm):
    sp = xp.shape[0] // t
    xs0 = jnp.zeros((n_rows * sp, LANES), xp.dtype)
    return pl.pallas_call(
        functools.partial(_dispatch_kernel, bm=bm, sp=sp),
        grid_spec=pltpu.PrefetchScalarGridSpec(
            num_scalar_prefetch=1,
            grid=(t // bm,),
            in_specs=[pl.BlockSpec((bm * sp, LANES), lambda i, pos: (i, 0)), pl.BlockSpec(memory_space=pl.ANY)],
            out_specs=pl.BlockSpec(memory_space=pl.ANY),
            scratch_shapes=[pltpu.SemaphoreType.DMA(())],
        ),
        out_shape=jax.ShapeDtypeStruct((n_rows * sp, LANES), xp.dtype),
        input_output_aliases={2: 0},
        compiler_params=_cparams("arbitrary"),
        name="moe_dispatch",
    )(pos_flat, xp, xs0)


def _expert_mlp_kernel(te_ref, nr_ref, xs_ref, w1g_ref, w1l_ref, b1g_ref, b1l_ref, w2_ref, b2_ref, ys_ref, xb_ref,
                       acc_ref, wg_ref, wl_ref, w2b_ref, *, tm, subs):
    i = pl.program_id(0)
    j = pl.program_id(1)
    nj = pl.num_programs(1)
    n_rows = nr_ref[i]
    d = acc_ref.shape[1]
    half = d // 2
    sp = half // LANES
    so = d // LANES

    for sub in range(subs):
        r0 = sub * tm

        @pl.when(jnp.logical_and(n_rows > r0, j == 0))
        def _():
            for s in range(sp):
                w = xs_ref[pl.ds(r0 * sp + s, tm, stride=sp), :]
                xb_ref[r0:r0 + tm, s * LANES:(s + 1) * LANES] = lax.bitcast_convert_type(w << 16, F32).astype(BF16)
                xb_ref[r0:r0 + tm, half + s * LANES:half + (s + 1) * LANES] = lax.bitcast_convert_type(
                    w & jnp.uint32(HI_MASK), F32).astype(BF16)
            acc_ref[r0:r0 + tm, :] = jnp.broadcast_to(b2_ref[0], (tm, acc_ref.shape[1]))

    def hidden_block(n_act, final):
        wg_ref[...] = w1g_ref[0].astype(BF16)
        wl_ref[...] = w1l_ref[0].astype(BF16)
        w2b_ref[...] = w2_ref[0].astype(BF16)
        xbs = [xb_ref[s * tm:(s + 1) * tm, :] for s in range(n_act)]
        ug = [jnp.dot(x, wg_ref[...], preferred_element_type=F32) + b1g_ref[0] for x in xbs]
        ul = [jnp.dot(x, wl_ref[...], preferred_element_type=F32) + b1l_ref[0] for x in xbs]
        acts = []
        for g, l in zip(ug, ul):
            x_glu = jnp.minimum(g, SWIGLU_LIMIT)
            x_lin = jnp.clip(l, -SWIGLU_LIMIT, SWIGLU_LIMIT)
            acts.append((x_glu * jax.nn.sigmoid(SWIGLU_ALPHA * x_glu) * (x_lin + 1.0)).astype(BF16))
        for s, a in enumerate(acts):
            r0 = s * tm
            y = acc_ref[r0:r0 + tm, :] + jnp.dot(a, w2b_ref[...], preferred_element_type=F32)
            if final:
                for c in range(so):
                    ys_ref[pl.ds(r0 * so + c, tm, stride=so), :] = y[:, c * LANES:(c + 1) * LANES]
            else:
                acc_ref[r0:r0 + tm, :] = y

    is_last = j == nj - 1
    for n_act in range(1, subs + 1):
        lo = n_rows > (n_act - 1) * tm
        cond = lo if n_act == subs else jnp.logical_and(lo, n_rows <= n_act * tm)
        pl.when(jnp.logical_and(cond, jnp.logical_not(is_last)))(functools.partial(hidden_block, n_act, False))
        pl.when(jnp.logical_and(cond, is_last))(functools.partial(hidden_block, n_act, True))

    for sub in range(subs):
        r0 = sub * tm
        active = n_rows > r0

        @pl.when(jnp.logical_and(jnp.logical_not(active), j == nj - 1))
        def _():
            ys_ref[r0 * so:(r0 + tm) * so, :] = jnp.zeros((tm * so, LANES), ys_ref.dtype)


def _expert_mlp(xs, tile_expert, tile_rows, w1, b1, w2, b2, tm, subs, th):
    n_we, d, two_h = w1.shape
    sp = d // (2 * LANES)
    so = d // LANES
    n_rows = xs.shape[0] // sp
    hdim = two_h // 2
    nj = hdim // th
    tb = tm * subs
    n_tiles = n_rows // tb
    b1r = b1.reshape(n_we, 1, two_h)
    b2r = b2.reshape(n_we, 1, d)

    def jj(i, j, nr):
        return jnp.where(nr[i] > 0, j, nj - 1)

    return pl.pallas_call(
        functools.partial(_expert_mlp_kernel, tm=tm, subs=subs),
        grid_spec=pltpu.PrefetchScalarGridSpec(
            num_scalar_prefetch=2,
            grid=(n_tiles, nj),
            in_specs=[
                pl.BlockSpec((tb * sp, LANES), lambda i, j, te, nr: (i, 0)),
                pl.BlockSpec((1, d, th), lambda i, j, te, nr: (te[i], 0, jj(i, j, nr))),
                pl.BlockSpec((1, d, th), lambda i, j, te, nr: (te[i], 0, nj + jj(i, j, nr))),
                pl.BlockSpec((1, 1, th), lambda i, j, te, nr: (te[i], 0, jj(i, j, nr))),
                pl.BlockSpec((1, 1, th), lambda i, j, te, nr: (te[i], 0, nj + jj(i, j, nr))),
                pl.BlockSpec((1, th, d), lambda i, j, te, nr: (te[i], jj(i, j, nr), 0)),
                pl.BlockSpec((1, 1, d), lambda i, j, te, nr: (te[i], 0, 0)),
            ],
            out_specs=pl.BlockSpec((tb * so, LANES), lambda i, j, te, nr: (i, 0)),
            scratch_shapes=[pltpu.VMEM((tb, d), BF16), pltpu.VMEM((tb, d), F32), pltpu.VMEM((d, th), BF16),
                            pltpu.VMEM((d, th), BF16), pltpu.VMEM((th, d), BF16)],
        ),
        out_shape=jax.ShapeDtypeStruct((n_rows * so, LANES), F32),
        compiler_params=_cparams("arbitrary", "arbitrary"),
        name="expert_mlp",
    )(tile_expert, tile_rows, xs, w1, w1, b1r, b1r, w2, b2r)


def _combine_ln_kernel(pos_ref, ys_ref, gate_ref, x_ref, g_ref, b_ref, o_ref, ob_ref, buf, acc_ref, sem, *, bm,
                       alpha):
    base = pl.program_id(0) * (bm * TOP_K)
    so = acc_ref.shape[1] // LANES

    hb = bm // COMBINE_PARTS

    def copy(r, k, part):
        return pltpu.make_async_copy(ys_ref.at[_token_rows(pos_ref[base + r * TOP_K + k], so), :],
                                     buf.at[k, _token_rows(r, so), :], sem.at[part])

    def start(part):
        def body(r, carry):
            for k in range(TOP_K):
                copy(r, k, part).start(priority=k % DMA_PRIORITIES)
            return carry
        return body

    def wait(part):
        def body(r, carry):
            for k in range(TOP_K):
                copy(r, k, part).wait()
            return carry
        return body

    for part in range(COMBINE_PARTS):
        lax.fori_loop(part * hb, (part + 1) * hb, start(part), 0)
    for part in range(COMBINE_PARTS):
        r0 = part * hb
        lax.fori_loop(r0, r0 + hb, wait(part), 0, unroll=2)
        gates = gate_ref[r0:r0 + hb, :]
        for s in range(so):
            acc = gates[:, 0:1] * buf[0, pl.ds(r0 * so + s, hb, stride=so), :]
            for k in range(1, TOP_K):
                acc = acc + gates[:, k:k + 1] * buf[k, pl.ds(r0 * so + s, hb, stride=so), :]
            acc_ref[r0:r0 + hb, s * LANES:(s + 1) * LANES] = acc
        y = _layer_norm_rows(alpha * x_ref[r0:r0 + hb, :] + acc_ref[r0:r0 + hb, :], g_ref[...], b_ref[...])
        o_ref[r0:r0 + hb, :] = y
        ob_ref[r0:r0 + hb, :] = y.astype(BF16)


def _combine_ln(ys, pos_flat, gates, x, ln_g, ln_b, alpha, bm):
    t, d = x.shape
    row = lambda i, pos: (i, 0)
    const = lambda i, pos: (0, 0)
    return pl.pallas_call(
        functools.partial(_combine_ln_kernel, bm=bm, alpha=alpha),
        grid_spec=pltpu.PrefetchScalarGridSpec(
            num_scalar_prefetch=1,
            grid=(t // bm,),
            in_specs=[
                pl.BlockSpec(memory_space=pl.ANY),
                pl.BlockSpec((bm, LANES), row),
                pl.BlockSpec((bm, d), row),
                pl.BlockSpec((1, d), const),
                pl.BlockSpec((1, d), const),
            ],
            out_specs=[pl.BlockSpec((bm, d), row), pl.BlockSpec((bm, d), row)],
            scratch_shapes=[pltpu.VMEM((TOP_K, bm * (d // LANES), LANES), F32), pltpu.VMEM((bm, d), F32),
                            pltpu.SemaphoreType.DMA((COMBINE_PARTS,))],
        ),
        out_shape=[jax.ShapeDtypeStruct((t, d), F32), jax.ShapeDtypeStruct((t, d), BF16)],
        compiler_params=_cparams("arbitrary"),
        name="moe_combine_ln",
    )(pos_flat, ys, gates, x, ln_g, ln_b)


def _moe_ln(x1, xp, idx, gates, layer, n_exp, w1, b1, w2, b2, ln_g, ln_b, alpha, tm, subs, th, bm_rank, bm_move):
    t, d = x1.shape
    tb = tm * subs
    rank, counts = _rank(idx, bm_rank)
    counts = counts[0, :n_exp]
    padded = ((counts + tb - 1) // tb) * tb
    ends = jnp.cumsum(padded)
    offs = ends - padded
    top = idx[:, :TOP_K]
    pos = (offs[top] + rank[:, :TOP_K]).reshape(-1).astype(jnp.int32)
    n_tiles = -(-(t * TOP_K) // tb) + n_exp
    starts = jnp.arange(n_tiles, dtype=jnp.int32) * tb
    tile_expert = jnp.minimum(jnp.sum((ends[None, :] <= starts[:, None]).astype(jnp.int32), axis=1), n_exp - 1)
    tile_rows = jnp.clip(counts[tile_expert] - (starts - offs[tile_expert]), 0, tb)
    tile_rows = jnp.where(starts < ends[-1], tile_rows, 0).astype(jnp.int32)
    last_valid = jnp.maximum(ends[-1] // tb - 1, 0)
    tile_expert = jnp.where(starts < ends[-1], tile_expert, tile_expert[last_valid])
    tile_expert = (tile_expert + layer * n_exp).astype(jnp.int32)
    xs = _dispatch(xp, pos, t, n_tiles * tb, bm_move)
    ys = _expert_mlp(xs, tile_expert, tile_rows, w1, b1, w2, b2, tm, subs, th)
    return _combine_ln(ys, pos, gates, x1, ln_g, ln_b, alpha, bm_move)


def _pad_lanes(v, offset=0, fill=0.0):
    out = jnp.full((1, LANES), fill, F32)
    return out.at[0, offset:offset + v.shape[0]].set(v.astype(F32))


def _tiles(t):
    def pick(cands):
        for c in cands:
            if t % c == 0:
                return c
        raise ValueError(t)
    big = t >= 4096
    return dict(bm_proj=pick((1024, 512, 256, 64)), bm_ln=pick((256, 64)), bm_rank=pick((512, 256, 64)),
                bm_move=pick((512, 256, 64)), tm=512 if big else 64, subs=2)


def kernel(x_prompt, x_sample, state_pool, state_conv, state_delta, w_in, conv_w, pool_w, pool_scale, a_log, dt_bias,
           dn_norm_w, w_out, ln1_g, ln1_b, w_router, b_router, w1, b1, w2, b2, ln2_g, ln2_b):
    bp, lp, d = x_prompt.shape
    bs, ls, _ = x_sample.shape
    depth = w_in.shape[0]
    n_exp = w_router.shape[-1]
    alpha = float((2 * depth) ** 0.25)
    tp, ts = bp * lp, bs * ls
    t = tp + ts
    tl = _tiles(t)
    c_p = CHUNK if lp % CHUNK == 0 else lp
    c_s = CHUNK if ls % CHUNK == 0 else ls
    th = 256
    assert lp >= POOL_HIST and ls >= POOL_HIST

    x = jnp.concatenate([x_prompt.reshape(tp, d), x_sample.reshape(ts, d)], axis=0)
    xb = x.astype(BF16)
    zero_pool = jnp.zeros((bp, POOL_PAD, POOL_WIDTH), F32)
    zero_conv = jnp.zeros((bp, CONV_PAD, QKV_WIDTH), F32)
    zero_state = jnp.zeros((bp, N_HEADS, HEAD_DIM, HEAD_DIM), F32)
    w1s = w1.reshape((depth * n_exp,) + w1.shape[2:])
    b1s = b1.reshape((depth * n_exp,) + b1.shape[2:])
    w2s = w2.reshape((depth * n_exp,) + w2.shape[2:])
    b2s = b2.reshape((depth * n_exp,) + b2.shape[2:])
    outs = {k: [] for k in ("pp", "pc", "pd", "sp", "sc", "sd")}
    for l in range(depth):
        w_main = w_in[l][:, :MAIN_WIDTH].astype(BF16)
        w_small = jnp.pad(w_in[l][:, MAIN_WIDTH:], ((0, 0), (0, LANES - 2 * N_HEADS))).astype(BF16)
        z = _matmul(xb, w_main, tl["bm_proj"], 512)
        zs = _matmul(xb, w_small, tl["bm_proj"], LANES)
        a_log_row = _pad_lanes(a_log[l], N_HEADS)
        dt_row = _pad_lanes(dt_bias[l], N_HEADS)
        seq_w = (conv_w[l], pool_w[l].astype(BF16), pool_scale[l].reshape(1, -1), a_log_row, dt_row,
                 dn_norm_w[l].reshape(1, -1))
        hp = jnp.pad(state_pool[l], ((0, 0), (POOL_PAD - POOL_HIST, 0), (0, 0)))
        hc = jnp.pad(state_conv[l], ((0, 0), (CONV_PAD - (CONV_W - 1), 0), (0, 0)))
        mix_p, sd_p, ps_p, cs_p = _seq_mixer(z, zs, 0, bp, lp, c_p, 0, zero_pool, zero_conv, zero_state, *seq_w)
        mix_s, sd_s, ps_s, cs_s = _seq_mixer(z, zs, tp, bs, ls, c_s, PAST_LEN, hp, hc, state_delta[l], *seq_w)
        mix = jnp.concatenate([mix_p, mix_s], axis=0)
        outs["pp"].append(ps_p[:, POOL_PAD - POOL_HIST:])
        outs["pc"].append(cs_p[:, CONV_PAD - (CONV_W - 1):])
        outs["pd"].append(sd_p)
        outs["sp"].append(ps_s[:, POOL_PAD - POOL_HIST:])
        outs["sc"].append(cs_s[:, CONV_PAD - (CONV_W - 1):])
        outs["sd"].append(sd_s)
        wr = jnp.pad(w_router[l], ((0, 0), (0, LANES - n_exp)))
        br = _pad_lanes(b_router[l])
        x1, xp, idx, gates = _outproj_ln_router(mix, w_out[l].astype(BF16), x, ln1_g[l].reshape(1, -1),
                                                ln1_b[l].reshape(1, -1), wr, br, alpha, n_exp, tl["bm_ln"])
        x, xb = _moe_ln(x1, xp, idx, gates, l, n_exp, w1s, b1s, w2s, b2s, ln2_g[l].reshape(1, -1),
                        ln2_b[l].reshape(1, -1), alpha, tl["tm"], tl["subs"], th, tl["bm_rank"], tl["bm_move"])
    y_prompt = x[:tp].reshape(bp, lp, d)
    y_sample = x[tp:].reshape(bs, ls, d)
    st = lambda k: jnp.stack(outs[k])
    return (y_prompt, y_sample, st("pp"), st("pc"), st("pd"), st("sp"), st("sc"), st("sd"))
```

```python
import functools

import jax
import jax.numpy as jnp
from jax import lax
from jax.experimental import pallas as pl
from jax.experimental.pallas import tpu as pltpu

F32 = jnp.float32
BF16 = jnp.bfloat16
U32 = jnp.uint32
HIGHEST = lax.Precision.HIGHEST

POOL_WINDOWS = (2, 4, 8, 16)
POOL_GROUP = 128
POOL_WIDTH = POOL_GROUP * len(POOL_WINDOWS)
POOL_HIST = max(POOL_WINDOWS) - 1
HEAD_DIM = 128
N_HEADS = 12
DN_WIDTH = N_HEADS * HEAD_DIM
QKV_WIDTH = 3 * DN_WIDTH
CONV_W = 4
MAIN_WIDTH = POOL_WIDTH + QKV_WIDTH + DN_WIDTH
GATE_COL = POOL_WIDTH + QKV_WIDTH
CHUNK = 64
PAST_LEN = 4096
TOP_K = 4
SWIGLU_LIMIT = 7.0
SWIGLU_ALPHA = 1.702
LN_EPS = 1e-5
RMS_EPS = 1e-6
L2_EPS = 1e-6

LANES = 128
SUBLANES = 8
SOLVE_BLOCK = 16
HEAD_GROUP = 4
POOL_PAD = 16
CONV_PAD = 8
VMEM_LIMIT = 56 * 1024 * 1024
HI_MASK = 0xFFFF0000
DMA_PRIORITIES = 2
COMBINE_PARTS = 2


def _cparams(*sem):
    return pltpu.CompilerParams(dimension_semantics=sem, vmem_limit_bytes=VMEM_LIMIT)


def _dotb(a, b):
    return jnp.dot(a.astype(BF16), b.astype(BF16), preferred_element_type=F32)


def _matmul_kernel(x_ref, w_ref, o_ref):
    o_ref[...] = jnp.dot(x_ref[...], w_ref[...], preferred_element_type=F32).astype(o_ref.dtype)


def _matmul(x, w, bm, bn, out_dtype=F32):
    m, k = x.shape
    n = w.shape[1]
    return pl.pallas_call(
        _matmul_kernel,
        grid=(m // bm, n // bn),
        in_specs=[pl.BlockSpec((bm, k), lambda i, j: (i, 0)), pl.BlockSpec((k, bn), lambda i, j: (0, j))],
        out_specs=pl.BlockSpec((bm, bn), lambda i, j: (i, j)),
        out_shape=jax.ShapeDtypeStruct((m, n), out_dtype),
        compiler_params=_cparams("parallel", "arbitrary"),
        name="in_proj",
    )(x, w)


def _unit_lower_inverses(ls, n, nb):
    r = lax.broadcasted_iota(jnp.int32, (n, n), 0)
    q = lax.broadcasted_iota(jnp.int32, (n, n), 1)
    eye = (r == q).astype(F32)
    same = (r // SOLVE_BLOCK) == (q // SOLVE_BLOCK)
    p = [jnp.where(same, l, 0.0) for l in ls]
    e = [jnp.where(same, 0.0, l) for l in ls]
    t_d = [eye - x for x in p]
    for _ in range(SOLVE_BLOCK.bit_length() - 2):
        p = [_dotb(x, x) for x in p]
        t_d = [t + _dotb(t, x) for t, x in zip(t_d, p)]
    if nb == 1:
        return t_d
    nn = [_dotb(t, x) for t, x in zip(t_d, e)]
    if nb == 2:
        return [t - _dotb(x, t) for t, x in zip(t_d, nn)]
    assert nb == 4, nb
    n2 = [_dotb(x, x) for x in nn]
    m1 = [t + _dotb(x, t) for t, x in zip(t_d, n2)]
    return [m - _dotb(x, m) for m, x in zip(m1, nn)]


def _seq_mixer_kernel(z_ref, zs_ref, hp_ref, hc_ref, s0_ref, convw_ref, poolw_ref, pscale_ref, alog_ref, dtb_ref,
                      nw_ref, mix_ref, s_ref, pstate_ref, cstate_ref, pext, cext, *, c, pos0):
    ci = pl.program_id(1)
    gh = HEAD_GROUP * c
    groups = range(N_HEADS // HEAD_GROUP)

    @pl.when(ci == 0)
    def _():
        pext[0:POOL_PAD, :] = hp_ref[0]
        cext[0:CONV_PAD, :] = hc_ref[0]
        s_ref[...] = s0_ref[...]

    pext[POOL_PAD:POOL_PAD + c, :] = z_ref[:, 0:POOL_WIDTH]
    cext[CONV_PAD:CONV_PAD + c, :] = z_ref[:, POOL_WIDTH:GATE_COL]

    pos = lax.broadcasted_iota(jnp.int32, (c, 1), 0) + ci * c + pos0
    for gi, w in enumerate(POOL_WINDOWS):
        cols = slice(gi * POOL_GROUP, (gi + 1) * POOL_GROUP)
        u = pext[POOL_PAD:POOL_PAD + c, cols]
        s = u
        for j in range(1, w):
            s = s + pext[POOL_PAD - j:POOL_PAD - j + c, cols]
        cnt = jnp.minimum(pos + 1, w).astype(F32)
        d = s / cnt - u
        y = jnp.dot(d.astype(BF16), poolw_ref[gi], preferred_element_type=F32)
        mix_ref[:, cols] = (y * pscale_ref[:, cols]).astype(mix_ref.dtype)

    zs = zs_ref[...]
    beta_all = jax.nn.sigmoid(zs)
    a = zs + dtb_ref[...]
    softplus = jnp.maximum(a, 0.0) + jnp.log1p(jnp.exp(-jnp.abs(a)))
    g_all = -jnp.exp(alog_ref[...]) * softplus
    rc = lax.broadcasted_iota(jnp.int32, (c, c), 0)
    qc = lax.broadcasted_iota(jnp.int32, (c, c), 1)
    gc_all = jnp.dot((rc >= qc).astype(F32), g_all, preferred_element_type=F32, precision=HIGHEST)

    def conv_block(col0):
        cols = slice(col0, col0 + HEAD_DIM)
        acc = convw_ref[0:1, cols] * cext[CONV_PAD - (CONV_W - 1):CONV_PAD - (CONV_W - 1) + c, cols]
        for j in range(1, CONV_W):
            lo = CONV_PAD - (CONV_W - 1) + j
            acc = acc + convw_ref[j:j + 1, cols] * cext[lo:lo + c, cols]
        return acc * jax.nn.sigmoid(acc)

    r2 = lax.broadcasted_iota(jnp.int32, (gh, gh), 0)
    q2 = lax.broadcasted_iota(jnp.int32, (gh, gh), 1)
    same_head = (r2 // c) == (q2 // c)
    incl = jnp.logical_and(same_head, r2 >= q2)
    strict = jnp.logical_and(same_head, r2 > q2)
    row_head = lax.broadcasted_iota(jnp.int32, (gh, LANES), 0) // c
    lane = lax.broadcasted_iota(jnp.int32, (gh, LANES), 1)
    gc_stack = jnp.concatenate([gc_all] * HEAD_GROUP, axis=0)
    beta_stack = jnp.concatenate([beta_all] * HEAD_GROUP, axis=0)
    ones_rows = jnp.ones((SUBLANES, LANES), F32)
    nt = (((1,), (1,)), ((), ()))
    tn = (((0,), (0,)), ((), ()))

    def stack(fn, g):
        return jnp.concatenate([fn(g * HEAD_GROUP + hh) for hh in range(HEAD_GROUP)], axis=0)

    def l2n(x):
        return x * lax.rsqrt(jnp.sum(x * x, -1, keepdims=True) + L2_EPS)

    q4 = [stack(lambda h: l2n(conv_block(h * HEAD_DIM)) * (HEAD_DIM ** -0.5), g) for g in groups]
    k4 = [stack(lambda h: l2n(conv_block(DN_WIDTH + h * HEAD_DIM)), g) for g in groups]
    v4 = [stack(lambda h: conv_block(2 * DN_WIDTH + h * HEAD_DIM), g) for g in groups]
    gsel = [jnp.where(lane == N_HEADS + g * HEAD_GROUP + row_head, gc_stack, 0.0) for g in groups]
    gcol = [jnp.sum(x, -1, keepdims=True) for x in gsel]
    grow = [lax.dot_general(ones_rows, x, nt, preferred_element_type=F32, precision=HIGHEST)[0:1, :] for x in gsel]
    beta = [jnp.sum(jnp.where(lane == g * HEAD_GROUP + row_head, beta_stack, 0.0), -1, keepdims=True) for g in groups]
    kb = [x.astype(BF16) for x in k4]
    qb = [x.astype(BF16) for x in q4]
    kk = [lax.dot_general(x, x, nt, preferred_element_type=F32) for x in kb]
    qk = [lax.dot_general(x, y, nt, preferred_element_type=F32) for x, y in zip(qb, kb)]
    dec = [jnp.exp(jnp.where(incl, gc - gr, -jnp.inf)) for gc, gr in zip(gcol, grow)]
    l_mat = [b * x * jnp.where(strict, d, 0.0) for b, x, d in zip(beta, kk, dec)]
    t_inv = _unit_lower_inverses(l_mat, gh, c // SOLVE_BLOCK)
    eg = [jnp.exp(x) for x in gcol]
    uw = [_dotb(t, jnp.concatenate([b * v, (b * e) * k], axis=1))
          for t, b, v, e, k in zip(t_inv, beta, v4, eg, k4)]
    rows = lambda x, hh: x[hh * c:(hh + 1) * c]
    s_old = [s_ref[0, h] for h in range(N_HEADS)]
    ws_qs = [[_dotb(jnp.concatenate([rows(uw[g], hh)[:, HEAD_DIM:], rows(q4[g], hh)], axis=0),
                    s_old[g * HEAD_GROUP + hh]) for hh in range(HEAD_GROUP)] for g in groups]
    u4 = [uw[g][:, :HEAD_DIM] - jnp.concatenate([x[:c] for x in ws_qs[g]], axis=0) for g in groups]
    qs4 = [jnp.concatenate([x[c:] for x in ws_qs[g]], axis=0) for g in groups]
    ub = [x.astype(BF16) for x in u4]
    o4 = [e * qs + jnp.dot((x * d).astype(BF16), u, preferred_element_type=F32)
          for e, qs, x, d, u in zip(eg, qs4, qk, dec, ub)]
    for g in groups:
        for hh in range(HEAD_GROUP):
            h = g * HEAD_GROUP + hh
            g_last = gc_all[c - 1:c, N_HEADS + h:N_HEADS + h + 1]
            tail = jnp.exp(g_last - rows(gcol[g], hh))
            s_ref[0, h] = jnp.exp(g_last) * s_old[h] + lax.dot_general(
                (rows(k4[g], hh) * tail).astype(BF16), rows(ub[g], hh), tn, preferred_element_type=F32)
            o = rows(o4[g], hh)
            o = o * lax.rsqrt(jnp.mean(o * o, -1, keepdims=True) + RMS_EPS) * nw_ref[...]
            gate = z_ref[:, GATE_COL + h * HEAD_DIM:GATE_COL + (h + 1) * HEAD_DIM]
            o = o * (gate * jax.nn.sigmoid(gate))
            mix_ref[:, POOL_WIDTH + h * HEAD_DIM:POOL_WIDTH + (h + 1) * HEAD_DIM] = o.astype(mix_ref.dtype)

    p_tail = pext[c:c + POOL_PAD, :]
    c_tail = cext[c:c + CONV_PAD, :]
    pext[0:POOL_PAD, :] = p_tail
    cext[0:CONV_PAD, :] = c_tail
    pstate_ref[0] = p_tail
    cstate_ref[0] = c_tail


def _seq_mixer(z, zs, row0, nseq, seqlen, c, pos0, hist_pool, hist_conv, s0, conv_w, pool_w, pool_scale, a_log_row,
               dt_bias_row, norm_w):
    nchunk = seqlen // c
    blk0 = row0 // c
    d_mix = POOL_WIDTH + DN_WIDTH
    row_map = lambda b, ci: (blk0 + b * nchunk + ci, 0)
    seq_map3 = lambda b, ci: (b, 0, 0)
    seq_map4 = lambda b, ci: (b, 0, 0, 0)
    const2 = lambda b, ci: (0, 0)
    kern = functools.partial(_seq_mixer_kernel, c=c, pos0=pos0)
    return pl.pallas_call(
        kern,
        grid=(nseq, nchunk),
        in_specs=[
            pl.BlockSpec((c, MAIN_WIDTH), row_map),
            pl.BlockSpec((c, LANES), row_map),
            pl.BlockSpec((1, POOL_PAD, POOL_WIDTH), seq_map3),
            pl.BlockSpec((1, CONV_PAD, QKV_WIDTH), seq_map3),
            pl.BlockSpec((1, N_HEADS, HEAD_DIM, HEAD_DIM), seq_map4),
            pl.BlockSpec((CONV_W, QKV_WIDTH), const2),
            pl.BlockSpec((len(POOL_WINDOWS), POOL_GROUP, POOL_GROUP), lambda b, ci: (0, 0, 0)),
            pl.BlockSpec((1, POOL_WIDTH), const2),
            pl.BlockSpec((1, LANES), const2),
            pl.BlockSpec((1, LANES), const2),
            pl.BlockSpec((1, HEAD_DIM), const2),
        ],
        out_specs=[
            pl.BlockSpec((c, d_mix), lambda b, ci: (b * nchunk + ci, 0)),
            pl.BlockSpec((1, N_HEADS, HEAD_DIM, HEAD_DIM), seq_map4),
            pl.BlockSpec((1, POOL_PAD, POOL_WIDTH), seq_map3),
            pl.BlockSpec((1, CONV_PAD, QKV_WIDTH), seq_map3),
        ],
        out_shape=[
            jax.ShapeDtypeStruct((nseq * seqlen, d_mix), BF16),
            jax.ShapeDtypeStruct((nseq, N_HEADS, HEAD_DIM, HEAD_DIM), F32),
            jax.ShapeDtypeStruct((nseq, POOL_PAD, POOL_WIDTH), F32),
            jax.ShapeDtypeStruct((nseq, CONV_PAD, QKV_WIDTH), F32),
        ],
        scratch_shapes=[pltpu.VMEM((POOL_PAD + c, POOL_WIDTH), F32), pltpu.VMEM((CONV_PAD + c, QKV_WIDTH), F32)],
        compiler_params=_cparams("parallel", "arbitrary"),
        name="seq_mixer",
    )(z, zs, hist_pool, hist_conv, s0, conv_w, pool_w, pool_scale, a_log_row, dt_bias_row, norm_w)


def _layer_norm_rows(x, g, b):
    xc = x - jnp.mean(x, -1, keepdims=True)
    var = jnp.mean(xc * xc, -1, keepdims=True)
    return xc * lax.rsqrt(var + LN_EPS) * g + b


def _pack_bf16_pairs(x):
    half = x.shape[1] // 2
    lo = lax.bitcast_convert_type(x[:, :half].astype(BF16).astype(F32), U32) >> 16
    hi = lax.bitcast_convert_type(x[:, half:].astype(BF16).astype(F32), U32) & jnp.uint32(HI_MASK)
    return lo | hi


def _outproj_ln_router_kernel(mix_ref, wout_ref, x_ref, g_ref, b_ref, wr_ref, br_ref, x1_ref, xp_ref, idx_ref,
                              gate_ref, *, alpha, n_experts):
    m = jnp.dot(mix_ref[...], wout_ref[...], preferred_element_type=F32)
    x1 = _layer_norm_rows(alpha * x_ref[...] + m, g_ref[...], b_ref[...])
    x1_ref[...] = x1
    words = _pack_bf16_pairs(x1)
    bm = x1.shape[0]
    sp = xp_ref.shape[0] // bm
    for s in range(sp):
        xp_ref[pl.ds(s, bm, stride=sp), :] = words[:, s * LANES:(s + 1) * LANES]
    logits = jnp.dot(x1, wr_ref[...], preferred_element_type=F32, precision=HIGHEST) + br_ref[...]
    lane = lax.broadcasted_iota(jnp.int32, logits.shape, 1).astype(F32)
    logits = jnp.where(lane < n_experts, logits, -jnp.inf)
    vals, idxs = [], []
    for _ in range(TOP_K):
        v = jnp.max(logits, -1, keepdims=True)
        i = jnp.min(jnp.where(logits == v, lane, float(LANES)), -1, keepdims=True)
        vals.append(v)
        idxs.append(i)
        logits = jnp.where(lane == i, -jnp.inf, logits)
    es = [jnp.exp(v - vals[0]) for v in vals]
    denom = es[0]
    for e in es[1:]:
        denom = denom + e
    idx_out = jnp.zeros(logits.shape, F32)
    gate_out = jnp.zeros(logits.shape, F32)
    for k in range(TOP_K):
        idx_out = jnp.where(lane == k, idxs[k], idx_out)
        gate_out = jnp.where(lane == k, es[k] / denom, gate_out)
    idx_ref[...] = idx_out.astype(jnp.int32)
    gate_ref[...] = gate_out


def _outproj_ln_router(mix, w_out, x, ln_g, ln_b, w_router_pad, b_router_pad, alpha, n_experts, bm):
    t, d = x.shape
    sp = d // (2 * LANES)
    row = lambda i: (i, 0)
    const = lambda i: (0, 0)
    kern = functools.partial(_outproj_ln_router_kernel, alpha=alpha, n_experts=n_experts)
    return pl.pallas_call(
        kern,
        grid=(t // bm,),
        in_specs=[
            pl.BlockSpec((bm, mix.shape[1]), row),
            pl.BlockSpec(w_out.shape, const),
            pl.BlockSpec((bm, d), row),
            pl.BlockSpec((1, d), const),
            pl.BlockSpec((1, d), const),
            pl.BlockSpec((d, LANES), const),
            pl.BlockSpec((1, LANES), const),
        ],
        out_specs=[pl.BlockSpec((bm, d), row), pl.BlockSpec((bm * sp, LANES), row),
                   pl.BlockSpec((bm, LANES), row), pl.BlockSpec((bm, LANES), row)],
        out_shape=[
            jax.ShapeDtypeStruct((t, d), F32),
            jax.ShapeDtypeStruct((t * sp, LANES), U32),
            jax.ShapeDtypeStruct((t, LANES), jnp.int32),
            jax.ShapeDtypeStruct((t, LANES), F32),
        ],
        compiler_params=_cparams("parallel"),
        name="outproj_ln_router",
    )(mix, w_out, x, ln_g, ln_b, w_router_pad, b_router_pad)


def _rank_kernel(idx_ref, rank_ref, count_ref, carry):
    @pl.when(pl.program_id(0) == 0)
    def _():
        carry[...] = jnp.zeros_like(carry)

    idx = idx_ref[...]
    bm = idx.shape[0]
    lane = lax.broadcasted_iota(jnp.int32, (bm, LANES), 1)
    onehots = [(idx[:, k:k + 1] == lane) for k in range(TOP_K)]
    cnt = onehots[0].astype(F32)
    for oh in onehots[1:]:
        cnt = cnt + oh.astype(F32)
    r = lax.broadcasted_iota(jnp.int32, (bm, bm), 0)
    q = lax.broadcasted_iota(jnp.int32, (bm, bm), 1)
    before = jnp.dot((r > q).astype(BF16), cnt.astype(BF16), preferred_element_type=F32) + carry[...]
    out = jnp.zeros((bm, LANES), jnp.int32)
    for k in range(TOP_K):
        rk = jnp.sum(jnp.where(onehots[k], before, 0.0), -1, keepdims=True).astype(jnp.int32)
        out = jnp.where(lane == k, rk, out)
    rank_ref[...] = out
    total = carry[...] + jnp.sum(cnt, 0, keepdims=True)
    carry[...] = total
    count_ref[...] = total.astype(jnp.int32)


def _rank(idx, bm):
    t = idx.shape[0]
    return pl.pallas_call(
        _rank_kernel,
        grid=(t // bm,),
        in_specs=[pl.BlockSpec((bm, LANES), lambda i: (i, 0))],
        out_specs=[pl.BlockSpec((bm, LANES), lambda i: (i, 0)), pl.BlockSpec((1, LANES), lambda i: (0, 0))],
        out_shape=[jax.ShapeDtypeStruct((t, LANES), jnp.int32), jax.ShapeDtypeStruct((1, LANES), jnp.int32)],
        scratch_shapes=[pltpu.VMEM((1, LANES), F32)],
        compiler_params=_cparams("arbitrary"),
        name="route_rank",
    )(idx)


def _token_rows(r, sp):
    return pl.ds(pl.multiple_of(r * sp, sp), sp)


def _dispatch_kernel(pos_ref, x_ref, xs_in_ref, xs_ref, sem, *, bm, sp):
    del xs_in_ref
    base = pl.program_id(0) * (bm * TOP_K)

    def copy(r, k):
        return pltpu.make_async_copy(x_ref.at[_token_rows(r, sp), :],
                                     xs_ref.at[_token_rows(pos_ref[base + r * TOP_K + k], sp), :], sem)

    def start(r, carry):
        for k in range(TOP_K):
            copy(r, k).start(priority=k % DMA_PRIORITIES)
        return carry

    def wait(r, carry):
        for k in range(TOP_K):
            copy(r, k).wait()
        return carry

    lax.fori_loop(0, bm, start, 0)
    lax.fori_loop(0, bm, wait, 0, unroll=2)


def _dispatch(xp, pos_flat, t, n_rows, bm):
    sp = xp.shape[0] // t
    xs0 = jnp.zeros((n_rows * sp, LANES), xp.dtype)
    return pl.pallas_call(
        functools.partial(_dispatch_kernel, bm=bm, sp=sp),
        grid_spec=pltpu.PrefetchScalarGridSpec(
            num_scalar_prefetch=1,
            grid=(t // bm,),
            in_specs=[pl.BlockSpec((bm * sp, LANES), lambda i, pos: (i, 0)), pl.BlockSpec(memory_space=pl.ANY)],
            out_specs=pl.BlockSpec(memory_space=pl.ANY),
            scratch_shapes=[pltpu.SemaphoreType.DMA(())],
        ),
        out_shape=jax.ShapeDtypeStruct((n_rows * sp, LANES), xp.dtype),
        input_output_aliases={2: 0},
        compiler_params=_cparams("arbitrary"),
        name="moe_dispatch",
    )(pos_flat, xp, xs0)


def _expert_mlp_kernel(te_ref, nr_ref, xs_ref, w1g_ref, w1l_ref, b1g_ref, b1l_ref, w2_ref, b2_ref, ys_ref, xb_ref,
                       acc_ref, wg_ref, wl_ref, w2b_ref, *, tm, subs):
    i = pl.program_id(0)
    j = pl.program_id(1)
    nj = pl.num_programs(1)
    n_rows = nr_ref[i]
    d = acc_ref.shape[1]
    half = d // 2
    sp = half // LANES
    so = d // LANES

    def unpack(r0):
        for s in range(sp):
            w = xs_ref[pl.ds(r0 * sp + s, tm, stride=sp), :]
            xb_ref[r0:r0 + tm, s * LANES:(s + 1) * LANES] = lax.bitcast_convert_type(w << 16, F32).astype(BF16)
            xb_ref[r0:r0 + tm, half + s * LANES:half + (s + 1) * LANES] = lax.bitcast_convert_type(
                w & jnp.uint32(HI_MASK), F32).astype(BF16)

    def hidden_block(n_act, first, final):
        if first:
            for s in range(n_act):
                unpack(s * tm)
        wg_ref[...] = w1g_ref[0].astype(BF16)
        wl_ref[...] = w1l_ref[0].astype(BF16)
        w2b_ref[...] = w2_ref[0].astype(BF16)
        xbs = [xb_ref[s * tm:(s + 1) * tm, :] for s in range(n_act)]
        ug = [jnp.dot(x, wg_ref[...], preferred_element_type=F32) + b1g_ref[0] for x in xbs]
        ul = [jnp.dot(x, wl_ref[...], preferred_element_type=F32) + b1l_ref[0] for x in xbs]
        acts = []
        for g, l in zip(ug, ul):
            x_glu = jnp.minimum(g, SWIGLU_LIMIT)
            x_lin = jnp.clip(l, -SWIGLU_LIMIT, SWIGLU_LIMIT)
            acts.append((x_glu * jax.nn.sigmoid(SWIGLU_ALPHA * x_glu) * (x_lin + 1.0)).astype(BF16))
        for s, a in enumerate(acts):
            r0 = s * tm
            prev = jnp.broadcast_to(b2_ref[0], (tm, d)) if first else acc_ref[r0:r0 + tm, :]
            y = prev + jnp.dot(a, w2b_ref[...], preferred_element_type=F32)
            if final:
                for c in range(so):
                    ys_ref[pl.ds(r0 * so + c, tm, stride=so), :] = y[:, c * LANES:(c + 1) * LANES]
            else:
                acc_ref[r0:r0 + tm, :] = y

    for n_act in range(1, subs + 1):
        lo = n_rows > (n_act - 1) * tm
        cond = lo if n_act == subs else jnp.logical_and(lo, n_rows <= n_act * tm)
        for first in (False, True):
            for final in (False, True):
                phase = jnp.logical_and((j == 0) == first, (j == nj - 1) == final)
                pl.when(jnp.logical_and(cond, phase))(functools.partial(hidden_block, n_act, first, final))

    for sub in range(subs):
        r0 = sub * tm
        active = n_rows > r0

        @pl.when(jnp.logical_and(jnp.logical_not(active), j == nj - 1))
        def _():
            ys_ref[r0 * so:(r0 + tm) * so, :] = jnp.zeros((tm * so, LANES), ys_ref.dtype)


def _expert_mlp(xs, tile_expert, tile_rows, w1, b1, w2, b2, tm, subs, th):
    n_we, d, two_h = w1.shape
    sp = d // (2 * LANES)
    so = d // LANES
    n_rows = xs.shape[0] // sp
    hdim = two_h // 2
    nj = hdim // th
    tb = tm * subs
    n_tiles = n_rows // tb
    b1r = b1.reshape(n_we, 1, two_h)
    b2r = b2.reshape(n_we, 1, d)

    def jj(i, j, nr):
        return jnp.where(nr[i] > 0, j, nj - 1)

    return pl.pallas_call(
        functools.partial(_expert_mlp_kernel, tm=tm, subs=subs),
        grid_spec=pltpu.PrefetchScalarGridSpec(
            num_scalar_prefetch=2,
            grid=(n_tiles, nj),
            in_specs=[
                pl.BlockSpec((tb * sp, LANES), lambda i, j, te, nr: (i, 0)),
                pl.BlockSpec((1, d, th), lambda i, j, te, nr: (te[i], 0, jj(i, j, nr))),
                pl.BlockSpec((1, d, th), lambda i, j, te, nr: (te[i], 0, nj + jj(i, j, nr))),
                pl.BlockSpec((1, 1, th), lambda i, j, te, nr: (te[i], 0, jj(i, j, nr))),
                pl.BlockSpec((1, 1, th), lambda i, j, te, nr: (te[i], 0, nj + jj(i, j, nr))),
                pl.BlockSpec((1, th, d), lambda i, j, te, nr: (te[i], jj(i, j, nr), 0)),
                pl.BlockSpec((1, 1, d), lambda i, j, te, nr: (te[i], 0, 0)),
            ],
            out_specs=pl.BlockSpec((tb * so, LANES), lambda i, j, te, nr: (i, 0)),
            scratch_shapes=[pltpu.VMEM((tb, d), BF16), pltpu.VMEM((tb, d), F32), pltpu.VMEM((d, th), BF16),
                            pltpu.VMEM((d, th), BF16), pltpu.VMEM((th, d), BF16)],
        ),
        out_shape=jax.ShapeDtypeStruct((n_rows * so, LANES), F32),
        compiler_params=_cparams("arbitrary", "arbitrary"),
        name="expert_mlp",
    )(tile_expert, tile_rows, xs, w1, w1, b1r, b1r, w2, b2r)


def _combine_ln_kernel(pos_ref, ys_ref, gate_ref, x_ref, g_ref, b_ref, o_ref, ob_ref, buf, acc_ref, sem, *, bm,
                       alpha):
    base = pl.program_id(0) * (bm * TOP_K)
    so = acc_ref.shape[1] // LANES

    hb = bm // COMBINE_PARTS

    def copy(r, k, part):
        return pltpu.make_async_copy(ys_ref.at[_token_rows(pos_ref[base + r * TOP_K + k], so), :],
                                     buf.at[k, _token_rows(r, so), :], sem.at[part])

    def start(part):
        def body(r, carry):
            for k in range(TOP_K):
                copy(r, k, part).start(priority=k % DMA_PRIORITIES)
            return carry
        return body

    def wait(part):
        def body(r, carry):
            for k in range(TOP_K):
                copy(r, k, part).wait()
            return carry
        return body

    for part in range(COMBINE_PARTS):
        lax.fori_loop(part * hb, (part + 1) * hb, start(part), 0)
    for part in range(COMBINE_PARTS):
        r0 = part * hb
        lax.fori_loop(r0, r0 + hb, wait(part), 0, unroll=2)
        gates = gate_ref[r0:r0 + hb, :]
        for s in range(so):
            acc = gates[:, 0:1] * buf[0, pl.ds(r0 * so + s, hb, stride=so), :]
            for k in range(1, TOP_K):
                acc = acc + gates[:, k:k + 1] * buf[k, pl.ds(r0 * so + s, hb, stride=so), :]
            acc_ref[r0:r0 + hb, s * LANES:(s + 1) * LANES] = acc
        y = _layer_norm_rows(alpha * x_ref[r0:r0 + hb, :] + acc_ref[r0:r0 + hb, :], g_ref[...], b_ref[...])
        o_ref[r0:r0 + hb, :] = y
        ob_ref[r0:r0 + hb, :] = y.astype(BF16)


def _combine_ln(ys, pos_flat, gates, x, ln_g, ln_b, alpha, bm):
    t, d = x.shape
    row = lambda i, pos: (i, 0)
    const = lambda i, pos: (0, 0)
    return pl.pallas_call(
        functools.partial(_combine_ln_kernel, bm=bm, alpha=alpha),
        grid_spec=pltpu.PrefetchScalarGridSpec(
            num_scalar_prefetch=1,
            grid=(t // bm,),
            in_specs=[
                pl.BlockSpec(memory_space=pl.ANY),
                pl.BlockSpec((bm, LANES), row),
                pl.BlockSpec((bm, d), row),
                pl.BlockSpec((1, d), const),
                pl.BlockSpec((1, d), const),
            ],
            out_specs=[pl.BlockSpec((bm, d), row), pl.BlockSpec((bm, d), row)],
            scratch_shapes=[pltpu.VMEM((TOP_K, bm * (d // LANES), LANES), F32), pltpu.VMEM((bm, d), F32),
                            pltpu.SemaphoreType.DMA((COMBINE_PARTS,))],
        ),
        out_shape=[jax.ShapeDtypeStruct((t, d), F32), jax.ShapeDtypeStruct((t, d), BF16)],
        compiler_params=_cparams("arbitrary"),
        name="moe_combine_ln",
    )(pos_flat, ys, gates, x, ln_g, ln_b)


def _moe_ln(x1, xp, idx, gates, layer, n_exp, w1, b1, w2, b2, ln_g, ln_b, alpha, tm, subs, th, bm_rank, bm_move):
    t, d = x1.shape
    tb = tm * subs
    rank, counts = _rank(idx, bm_rank)
    counts = counts[0, :n_exp]
    padded = ((counts + tb - 1) // tb) * tb
    ends = jnp.cumsum(padded)
    offs = ends - padded
    top = idx[:, :TOP_K]
    pos = (offs[top] + rank[:, :TOP_K]).reshape(-1).astype(jnp.int32)
    n_tiles = -(-(t * TOP_K) // tb) + n_exp
    starts = jnp.arange(n_tiles, dtype=jnp.int32) * tb
    tile_expert = jnp.minimum(jnp.sum((ends[None, :] <= starts[:, None]).astype(jnp.int32), axis=1), n_exp - 1)
    tile_rows = jnp.clip(counts[tile_expert] - (starts - offs[tile_expert]), 0, tb)
    tile_rows = jnp.where(starts < ends[-1], tile_rows, 0).astype(jnp.int32)
    last_valid = jnp.maximum(ends[-1] // tb - 1, 0)
    tile_expert = jnp.where(starts < ends[-1], tile_expert, tile_expert[last_valid])
    tile_expert = (tile_expert + layer * n_exp).astype(jnp.int32)
    xs = _dispatch(xp, pos, t, n_tiles * tb, bm_move)
    ys = _expert_mlp(xs, tile_expert, tile_rows, w1, b1, w2, b2, tm, subs, th)
    return _combine_ln(ys, pos, gates, x1, ln_g, ln_b, alpha, bm_move)


def _pad_lanes(v, offset=0, fill=0.0):
    out = jnp.full((1, LANES), fill, F32)
    return out.at[0, offset:offset + v.shape[0]].set(v.astype(F32))


def _tiles(t):
    def pick(cands):
        for c in cands:
            if t % c == 0:
                return c
        raise ValueError(t)
    big = t >= 4096
    return dict(bm_proj=pick((1024, 512, 256, 64)), bm_ln=pick((256, 64)), bm_rank=pick((512, 256, 64)),
                bm_move=pick((512, 256, 64)), tm=512 if big else 64, subs=2)


def kernel(x_prompt, x_sample, state_pool, state_conv, state_delta, w_in, conv_w, pool_w, pool_scale, a_log, dt_bias,
           dn_norm_w, w_out, ln1_g, ln1_b, w_router, b_router, w1, b1, w2, b2, ln2_g, ln2_b):
    bp, lp, d = x_prompt.shape
    bs, ls, _ = x_sample.shape
    depth = w_in.shape[0]
    n_exp = w_router.shape[-1]
    alpha = float((2 * depth) ** 0.25)
    tp, ts = bp * lp, bs * ls
    t = tp + ts
    tl = _tiles(t)
    c_p = CHUNK if lp % CHUNK == 0 else lp
    c_s = CHUNK if ls % CHUNK == 0 else ls
    th = 256
    assert lp >= POOL_HIST and ls >= POOL_HIST

    x = jnp.concatenate([x_prompt.reshape(tp, d), x_sample.reshape(ts, d)], axis=0)
    xb = x.astype(BF16)
    zero_pool = jnp.zeros((bp, POOL_PAD, POOL_WIDTH), F32)
    zero_conv = jnp.zeros((bp, CONV_PAD, QKV_WIDTH), F32)
    zero_state = jnp.zeros((bp, N_HEADS, HEAD_DIM, HEAD_DIM), F32)
    w1s = w1.reshape((depth * n_exp,) + w1.shape[2:])
    b1s = b1.reshape((depth * n_exp,) + b1.shape[2:])
    w2s = w2.reshape((depth * n_exp,) + w2.shape[2:])
    b2s = b2.reshape((depth * n_exp,) + b2.shape[2:])
    outs = {k: [] for k in ("pp", "pc", "pd", "sp", "sc", "sd")}
    for l in range(depth):
        w_main = w_in[l][:, :MAIN_WIDTH].astype(BF16)
        w_small = jnp.pad(w_in[l][:, MAIN_WIDTH:], ((0, 0), (0, LANES - 2 * N_HEADS))).astype(BF16)
        z = _matmul(xb, w_main, tl["bm_proj"], 512)
        zs = _matmul(xb, w_small, tl["bm_proj"], LANES)
        a_log_row = _pad_lanes(a_log[l], N_HEADS)
        dt_row = _pad_lanes(dt_bias[l], N_HEADS)
        seq_w = (conv_w[l], pool_w[l].astype(BF16), pool_scale[l].reshape(1, -1), a_log_row, dt_row,
                 dn_norm_w[l].reshape(1, -1))
        hp = jnp.pad(state_pool[l], ((0, 0), (POOL_PAD - POOL_HIST, 0), (0, 0)))
        hc = jnp.pad(state_conv[l], ((0, 0), (CONV_PAD - (CONV_W - 1), 0), (0, 0)))
        mix_p, sd_p, ps_p, cs_p = _seq_mixer(z, zs, 0, bp, lp, c_p, 0, zero_pool, zero_conv, zero_state, *seq_w)
        mix_s, sd_s, ps_s, cs_s = _seq_mixer(z, zs, tp, bs, ls, c_s, PAST_LEN, hp, hc, state_delta[l], *seq_w)
        mix = jnp.concatenate([mix_p, mix_s], axis=0)
        outs["pp"].append(ps_p[:, POOL_PAD - POOL_HIST:])
        outs["pc"].append(cs_p[:, CONV_PAD - (CONV_W - 1):])
        outs["pd"].append(sd_p)
        outs["sp"].append(ps_s[:, POOL_PAD - POOL_HIST:])
        outs["sc"].append(cs_s[:, CONV_PAD - (CONV_W - 1):])
        outs["sd"].append(sd_s)
        wr = jnp.pad(w_router[l], ((0, 0), (0, LANES - n_exp)))
        br = _pad_lanes(b_router[l])
        x1, xp, idx, gates = _outproj_ln_router(mix, w_out[l].astype(BF16), x, ln1_g[l].reshape(1, -1),
                                                ln1_b[l].reshape(1, -1), wr, br, alpha, n_exp, tl["bm_ln"])
        x, xb = _moe_ln(x1, xp, idx, gates, l, n_exp, w1s, b1s, w2s, b2s, ln2_g[l].reshape(1, -1),
                        ln2_b[l].reshape(1, -1), alpha, tl["tm"], tl["subs"], th, tl["bm_rank"], tl["bm_move"])
    y_prompt = x[:tp].reshape(bp, lp, d)
    y_sample = x[tp:].reshape(bs, ls, d)
    st = lambda k: jnp.stack(outs[k])
    return (y_prompt, y_sample, st("pp"), st("pc"), st("pd"), st("sp"), st("sc"), st("sd"))
```
